```python
import math
import jax, jax.numpy as jnp
from jax import lax
import numpy as np

D_MODEL = 1024
BATCH = 8
SEQ = 2048
DEPTH = 4

POOL_WIDTH = 512
POOL_WINDOWS = (2, 4, 8, 16)
POOL_GROUPS = len(POOL_WINDOWS)
POOL_GROUP_DIM = POOL_WIDTH // POOL_GROUPS
GLA_HEADS = 4
GLA_DK = 64
GLA_DV = 128
GLA_KEY_WIDTH = GLA_HEADS * GLA_DK
GLA_VAL_WIDTH = GLA_HEADS * GLA_DV
GLA_GATE_RANK = 16
GLA_GATE_TAU = 16.0
GLA_CHUNK = 64
MLA_HEADS = 8
MLA_Q_RANK = 384
MLA_KV_RANK = 256
MLA_NOPE = 64
MLA_ROPE = 32
MLA_V = 64
MLA_QK = MLA_NOPE + MLA_ROPE
MLA_VAL_WIDTH = MLA_HEADS * MLA_V
ROPE_BASE = 10000.0
Q_BLOCK = 128
N_BRANCH = 3
D_FF = -(-8 * D_MODEL // (3 * 256)) * 256
EPS = 1e-6

IN_SIZES = (POOL_WIDTH, GLA_KEY_WIDTH, GLA_KEY_WIDTH, GLA_VAL_WIDTH, GLA_VAL_WIDTH, GLA_GATE_RANK,
            MLA_Q_RANK, MLA_KV_RANK, MLA_ROPE, N_BRANCH * D_MODEL)
IN_WIDTH = sum(IN_SIZES)
IN_OFFSETS = tuple(int(v) for v in np.cumsum(IN_SIZES)[:-1])

kernel_name = "hybrid_pool_gla_mla_sandwich"


def rms_norm(x, g):
    xf = x.astype(jnp.float32)
    y = xf * lax.rsqrt(jnp.mean(xf * xf, axis=-1, keepdims=True) + EPS)
    return (y * g.astype(jnp.float32)).astype(x.dtype)


def rope_tables(positions):
    inv_freq = ROPE_BASE ** (-jnp.arange(0, MLA_ROPE, 2, dtype=jnp.float32) / MLA_ROPE)
    ang = positions.astype(jnp.float32)[..., None] * inv_freq
    return jnp.cos(ang), jnp.sin(ang)


def apply_rope(x, cos, sin):
    half = x.shape[-1] // 2
    x1, x2 = x[..., :half], x[..., half:]
    return jnp.concatenate([x1 * cos - x2 * sin, x2 * cos + x1 * sin], axis=-1).astype(x.dtype)


def pool_mixer(u, w_pool, pool_scale):
    b, s, _ = u.shape
    uf = u.astype(jnp.float32).reshape(b, s, POOL_GROUPS, POOL_GROUP_DIM)
    cs = jnp.cumsum(uf, axis=1)
    t = jnp.arange(s)
    means = []
    for g, w in enumerate(POOL_WINDOWS):
        c = cs[:, :, g]
        prev = jnp.pad(c, ((0, 0), (w, 0), (0, 0)))[:, :s]
        cnt = jnp.minimum(t + 1, w).astype(jnp.float32)[None, :, None]
        means.append((c - prev) / cnt)
    pooled = jnp.stack(means, axis=2)
    diff = (pooled - uf).astype(u.dtype)
    y = jnp.einsum('bsgc,gcd->bsgd', diff, w_pool).reshape(b, s, POOL_WIDTH)
    return y * pool_scale


def gla_mixer(q, k, v, r, a1, w_a2, b_a, g_norm):
    b, s, _ = q.shape
    n = s // GLA_CHUNK
    f32 = jnp.float32
    log_a = jax.nn.log_sigmoid((a1 @ w_a2 + b_a).astype(f32)) / GLA_GATE_TAU

    def heads(t, d):
        return t.astype(f32).reshape(b, n, GLA_CHUNK, GLA_HEADS, d).transpose(0, 3, 1, 2, 4)

    qh = heads(q, GLA_DK) * (GLA_DK ** -0.5)
    kh = heads(k, GLA_DK)
    vh = heads(v, GLA_DV)
    cum = jnp.cumsum(heads(log_a, GLA_DK), axis=3)
    cum_last = cum[:, :, :, -1:, :]
    q_dec = qh * jnp.exp(cum)
    k_inv = kh * jnp.exp(-cum)
    k_end = kh * jnp.exp(cum_last - cum)
    mask = jnp.tril(jnp.ones((GLA_CHUNK, GLA_CHUNK), dtype=bool))
    att = jnp.where(mask, jnp.einsum('bhntd,bhnsd->bhnts', q_dec, k_inv), 0.0)
    o_intra = jnp.einsum('bhnts,bhnse->bhnte', att, vh)
    kv = jnp.einsum('bhnsd,bhnse->bhnde', k_end, vh)
    decay = jnp.exp(cum_last[:, :, :, 0, :])

    def step(state, inp):
        dec, kv_n = inp
        return dec[..., None] * state + kv_n, state

    s0 = jnp.zeros((b, GLA_HEADS, GLA_DK, GLA_DV), f32)
    _, states = lax.scan(step, s0, (jnp.moveaxis(decay, 2, 0), jnp.moveaxis(kv, 2, 0)))
    states = jnp.moveaxis(states, 0, 2)
    o_inter = jnp.einsum('bhntd,bhnde->bhnte', q_dec, states)
    o = (o_intra + o_inter).transpose(0, 2, 3, 1, 4).reshape(b, s, GLA_HEADS, GLA_DV)
    o = o * lax.rsqrt(jnp.mean(o * o, axis=-1, keepdims=True) + EPS)
    o = o * g_norm.astype(f32).reshape(GLA_HEADS, GLA_DV)
    o = o.reshape(b, s, GLA_VAL_WIDTH) * jax.nn.silu(r.astype(f32))
    return o.astype(q.dtype)


def mla_mixer(cq, ckv, kr, cos, sin, q_norm, w_uq, kv_norm, w_ukv):
    b, s, _ = cq.shape
    qf = (rms_norm(cq, q_norm) @ w_uq).reshape(b, s, MLA_HEADS, MLA_QK)
    kvf = (rms_norm(ckv, kv_norm) @ w_ukv).reshape(b, s, MLA_HEADS, MLA_NOPE + MLA_V)
    q_nope, q_rope = qf[..., :MLA_NOPE], qf[..., MLA_NOPE:]
    k_nope, v = kvf[..., :MLA_NOPE], kvf[..., MLA_NOPE:]
    q_rope = apply_rope(q_rope, cos[:, :, None], sin[:, :, None])
    k_rope = apply_rope(kr, cos, sin)
    q = jnp.concatenate([q_nope, q_rope], axis=-1)
    k = jnp.concatenate([k_nope, jnp.broadcast_to(k_rope[:, :, None], (b, s, MLA_HEADS, MLA_ROPE))], axis=-1)
    nb = s // Q_BLOCK
    qb = q.reshape(b, nb, Q_BLOCK, MLA_HEADS, MLA_QK).transpose(1, 0, 2, 3, 4)
    starts = jnp.arange(nb) * Q_BLOCK
    kpos = jnp.arange(s)
    scale = MLA_QK ** -0.5

    def attend(args):
        q_blk, start = args
        sc = jnp.einsum('bqhd,bkhd->bhqk', q_blk, k).astype(jnp.float32) * scale
        qpos = start + jnp.arange(Q_BLOCK)
        sc = jnp.where(qpos[:, None] >= kpos[None, :], sc, jnp.finfo(jnp.float32).min)
        p = jax.nn.softmax(sc, axis=-1)
        return jnp.einsum('bhqk,bkhe->bqhe', p.astype(v.dtype), v)

    o = lax.map(attend, (qb, starts))
    return o.transpose(1, 0, 2, 3, 4).reshape(b, s, MLA_VAL_WIDTH)


def hybrid_layer(x, cos, sin, n_pre_mix, n_post_mix, n_pre_ffn, n_post_ffn, w_in, w_pool, pool_scale, w_a,
                 w_gla_a2, b_gla_a, gla_norm, w_b, mla_q_norm, w_mla_uq, mla_kv_norm, w_mla_ukv, w_c, w_o,
                 w_ffn_gu, w_ffn_down):
    b, s, d = x.shape
    h = rms_norm(x, n_pre_mix)
    proj = h @ w_in
    (u_pool, g_q, g_k, g_v, g_r, g_a1, m_cq, m_ckv, m_kr, gate_logits) = jnp.split(proj, IN_OFFSETS, axis=-1)
    y_a = pool_mixer(u_pool, w_pool, pool_scale) @ w_a
    y_b = gla_mixer(g_q, g_k, g_v, g_r, g_a1, w_gla_a2, b_gla_a, gla_norm) @ w_b
    y_c = mla_mixer(m_cq, m_ckv, m_kr, cos, sin, mla_q_norm, w_mla_uq, mla_kv_norm, w_mla_ukv) @ w_c
    gates = jax.nn.sigmoid(gate_logits.astype(jnp.float32)).astype(x.dtype).reshape(b, s, N_BRANCH, d)
    merged = gates[:, :, 0] * y_a + gates[:, :, 1] * y_b + gates[:, :, 2] * y_c
    x = x + rms_norm(merged @ w_o, n_post_mix)
    h = rms_norm(x, n_pre_ffn)
    gu = h @ w_ffn_gu
    ffn = (jax.nn.silu(gu[..., :D_FF]) * gu[..., D_FF:]) @ w_ffn_down
    return x + rms_norm(ffn, n_post_ffn)


def setup_inputs(seed: int = 0) -> dict:
    key = jax.random.key(seed)
    ks = jax.random.split(key, 24)
    f32 = jnp.float32

    def dense(k, shape, fan_in):
        return jax.random.normal(k, shape, f32) * (fan_in ** -0.5)

    def gain(k, shape):
        return 1.0 + 0.02 * jax.random.normal(k, shape, f32)

    x = jax.random.normal(ks[0], (BATCH, SEQ, D_MODEL), f32)
    offsets = jax.random.randint(ks[1], (BATCH, 1), 0, 4096, dtype=jnp.int32)
    positions = offsets + jnp.arange(SEQ, dtype=jnp.int32)[None, :]
    return {
        "x": x,
        "positions": positions,
        "norm_pre_mix": gain(ks[2], (DEPTH, D_MODEL)),
        "norm_post_mix": gain(ks[3], (DEPTH, D_MODEL)),
        "norm_pre_ffn": gain(ks[4], (DEPTH, D_MODEL)),
        "norm_post_ffn": gain(ks[5], (DEPTH, D_MODEL)),
        "w_in": dense(ks[6], (DEPTH, D_MODEL, IN_WIDTH), D_MODEL),
        "w_pool": dense(ks[7], (DEPTH, POOL_GROUPS, POOL_GROUP_DIM, POOL_GROUP_DIM), POOL_GROUP_DIM),
        "pool_scale": gain(ks[8], (DEPTH, POOL_WIDTH)),
        "w_a": dense(ks[9], (DEPTH, POOL_WIDTH, D_MODEL), POOL_WIDTH),
        "w_gla_a2": dense(ks[10], (DEPTH, GLA_GATE_RANK, GLA_KEY_WIDTH), GLA_GATE_RANK),
        "b_gla_a": 0.1 * jax.random.normal(ks[11], (DEPTH, GLA_KEY_WIDTH), f32),
        "gla_norm": gain(ks[12], (DEPTH, GLA_VAL_WIDTH)),
        "w_b": dense(ks[13], (DEPTH, GLA_VAL_WIDTH, D_MODEL), GLA_VAL_WIDTH),
        "mla_q_norm": gain(ks[14], (DEPTH, MLA_Q_RANK)),
        "w_mla_uq": dense(ks[15], (DEPTH, MLA_Q_RANK, MLA_HEADS * MLA_QK), MLA_Q_RANK),
        "mla_kv_norm": gain(ks[16], (DEPTH, MLA_KV_RANK)),
        "w_mla_ukv": dense(ks[17], (DEPTH, MLA_KV_RANK, MLA_HEADS * (MLA_NOPE + MLA_V)), MLA_KV_RANK),
        "w_c": dense(ks[18], (DEPTH, MLA_VAL_WIDTH, D_MODEL), MLA_VAL_WIDTH),
        "w_o": dense(ks[19], (DEPTH, D_MODEL, D_MODEL), D_MODEL),
        "w_ffn_gu": dense(ks[20], (DEPTH, D_MODEL, 2 * D_FF), D_MODEL),
        "w_ffn_down": dense(ks[21], (DEPTH, D_FF, D_MODEL), D_FF),
    }


def reference(x, positions, norm_pre_mix, norm_post_mix, norm_pre_ffn, norm_post_ffn, w_in, w_pool,
              pool_scale, w_a, w_gla_a2, b_gla_a, gla_norm, w_b, mla_q_norm, w_mla_uq, mla_kv_norm,
              w_mla_ukv, w_c, w_o, w_ffn_gu, w_ffn_down):
    cos, sin = rope_tables(positions)
    for l in range(DEPTH):
        x = hybrid_layer(x, cos, sin, norm_pre_mix[l], norm_post_mix[l], norm_pre_ffn[l], norm_post_ffn[l],
                         w_in[l], w_pool[l], pool_scale[l], w_a[l], w_gla_a2[l], b_gla_a[l], gla_norm[l],
                         w_b[l], mla_q_norm[l], w_mla_uq[l], mla_kv_norm[l], w_mla_ukv[l], w_c[l], w_o[l],
                         w_ffn_gu[l], w_ffn_down[l])
    return x
```

```python
import functools

import jax
import jax.numpy as jnp
from jax import lax
from jax.experimental import pallas as pl
from jax.experimental.pallas import tpu as pltpu

F32 = jnp.float32
BF16 = jnp.bfloat16

D_MODEL = 1024
DEPTH = 4
POOL_WIDTH = 512
POOL_WINDOWS = (2, 4, 8, 16)
POOL_GROUP_DIM = 128
GLA_HEADS = 4
GLA_DK = 64
GLA_DV = 128
GLA_KEY_WIDTH = GLA_HEADS * GLA_DK
GLA_VAL_WIDTH = GLA_HEADS * GLA_DV
GLA_GATE_RANK = 16
GLA_GATE_TAU = 16.0
GLA_CHUNK = 64
MLA_HEADS = 8
MLA_Q_RANK = 384
MLA_KV_RANK = 256
MLA_NOPE = 64
MLA_ROPE = 32
MLA_V = 64
MLA_QK = MLA_NOPE + MLA_ROPE
MLA_VAL_WIDTH = MLA_HEADS * MLA_V
ROPE_BASE = 10000.0
N_BRANCH = 3
D_FF = 2816
EPS = 1e-6

LANES = 128
HEAD_PAD = 128
VMEM_LIMIT = 56 * 1024 * 1024

IN_SEGS = (
    ("pool", POOL_WIDTH, F32),
    ("q", GLA_KEY_WIDTH, BF16),
    ("k", GLA_KEY_WIDTH, BF16),
    ("v", GLA_VAL_WIDTH, BF16),
    ("r", GLA_VAL_WIDTH, BF16),
    ("cq", MLA_Q_RANK, BF16),
    ("ckv", MLA_KV_RANK, BF16),
    ("misc", LANES, BF16),
    ("glog", N_BRANCH * D_MODEL, BF16),
)
IN_WIDTH_R = sum(s[1] for s in IN_SEGS)


def _dot(a, b):
    return jnp.dot(a, b, preferred_element_type=F32)


def _dot_nt(a, b):
    return lax.dot_general(a, b, (((1,), (1,)), ((), ())), preferred_element_type=F32)


def _dot_tn(a, b):
    return lax.dot_general(a, b, (((0,), (0,)), ((), ())), preferred_element_type=F32)


def _rms(x, g):
    return x * lax.rsqrt(jnp.mean(x * x, axis=-1, keepdims=True) + EPS) * g


def _sigmoid(x):
    return 0.5 * jnp.tanh(0.5 * x) + 0.5


def _params(n_axes):
    return pltpu.CompilerParams(
        dimension_semantics=("arbitrary",) * n_axes, vmem_limit_bytes=VMEM_LIMIT)


def _const_spec(shape):
    nd = len(shape)
    return pl.BlockSpec(shape, lambda *_: (0,) * nd, pipeline_mode=pl.Buffered(1))


def _rope_kernel(pos_ref, invf_ref, cos_ref, sin_ref):
    ang = pos_ref[...].astype(F32) * invf_ref[...]
    cos_ref[...] = jnp.cos(ang)
    sin_ref[...] = jnp.sin(ang)


def _rope_tables(positions):
    t = positions.size
    half = MLA_ROPE // 2
    inv_freq = ROPE_BASE ** (-jnp.arange(0, MLA_ROPE, 2, dtype=F32) / MLA_ROPE)
    rows = t * half // LANES
    pos_d = jnp.repeat(positions.reshape(t), half).reshape(rows, LANES)
    invf_d = jnp.tile(inv_freq, LANES // half).reshape(1, LANES)
    cos_d, sin_d = pl.pallas_call(
        _rope_kernel,
        out_shape=(jax.ShapeDtypeStruct((rows, LANES), F32),) * 2,
        name="rope_tables",
    )(pos_d, invf_d)
    cos = cos_d.reshape(t, half)
    sin = sin_d.reshape(t, half)
    one = jnp.ones((t, 32), F32)
    zero = jnp.zeros((t, half), F32)
    zero32 = jnp.zeros((t, 32), F32)
    c_tab = jnp.concatenate([one, cos, zero, one, cos, zero], axis=1)
    s_tab = jnp.concatenate([zero32, -sin, zero, zero32, sin, zero], axis=1)
    return c_tab, s_tab


def _in_proj_kernel(x_ref, g_ref, w_ref, *out_refs):
    h = _rms(x_ref[...], g_ref[...]).astype(BF16)
    c0 = 0
    for out_ref, (_, width, dtype) in zip(out_refs, IN_SEGS):
        for a in range(0, width, 1024):
            b = min(a + 1024, width)
            out_ref[:, a:b] = _dot(h, w_ref[:, c0 + a:c0 + b]).astype(dtype)
        c0 += width


def _in_proj(x2, g, w, tm=512):
    t = x2.shape[0]
    out_shape = tuple(jax.ShapeDtypeStruct((t, wd), dt) for _, wd, dt in IN_SEGS)
    out_specs = tuple(pl.BlockSpec((tm, wd), lambda i: (i, 0)) for _, wd, _ in IN_SEGS)
    return pl.pallas_call(
        _in_proj_kernel,
        grid=(t // tm,),
        in_specs=[
            pl.BlockSpec((tm, D_MODEL), lambda i: (i, 0)),
            _const_spec((1, D_MODEL)),
            _const_spec((D_MODEL, IN_WIDTH_R)),
        ],
        out_specs=out_specs,
        out_shape=out_shape,
        compiler_params=_params(1),
        name="in_proj",
    )(x2, g, w)


def _pool_kernel(u_ref, wp_ref, ps_ref, o_ref):
    s_len = u_ref.shape[0]
    t = lax.broadcasted_iota(jnp.int32, (s_len, POOL_GROUP_DIM), 0)
    for g, w in enumerate(POOL_WINDOWS):
        cols = slice(g * POOL_GROUP_DIM, (g + 1) * POOL_GROUP_DIM)
        u = u_ref[:, cols]
        s = u
        sh = 1
        while sh < w:
            s = s + jnp.where(t >= sh, pltpu.roll(s, sh, axis=0), 0.0)
            sh *= 2
        cnt = jnp.minimum(t + 1, w).astype(F32)
        diff = s / cnt - u
        y = _dot(diff.astype(BF16), wp_ref[g]) * ps_ref[:, cols]
        o_ref[:, cols] = y.astype(BF16)


def _pool(u, wp, ps, batch, seq):
    t = u.shape[0]
    return pl.pallas_call(
        _pool_kernel,
        grid=(batch,),
        in_specs=[
            pl.BlockSpec((seq, POOL_WIDTH), lambda b: (b, 0)),
            _const_spec(wp.shape),
            _const_spec((1, POOL_WIDTH)),
        ],
        out_specs=pl.BlockSpec((seq, POOL_WIDTH), lambda b: (b, 0)),
        out_shape=jax.ShapeDtypeStruct((t, POOL_WIDTH), BF16),
        compiler_params=_params(1),
        name="pool",
    )(u, wp, ps)


def _gla_kernel(q_ref, k_ref, v_ref, r_ref, misc_ref, wa2_ref, ba_ref, gn_ref, o_ref):
    s_len = q_ref.shape[0]
    c = GLA_CHUNK
    row = lax.broadcasted_iota(jnp.int32, (c, c), 0)
    col = lax.broadcasted_iota(jnp.int32, (c, c), 1)
    tril = row >= col
    tri_bf = tril.astype(BF16)
    lane = lax.broadcasted_iota(jnp.int32, (1, 2 * GLA_DK), 1)

    def body(n, st):
        rows = pl.ds(pl.multiple_of(n * c, c), c)
        z = _dot(misc_ref[rows, :], wa2_ref[...]) + ba_ref[...]
        la = (jnp.minimum(z, 0.0) - jnp.log1p(jnp.exp(-jnp.abs(z)))) * (1.0 / GLA_GATE_TAU)
        hi = la.astype(BF16)
        r1 = la - hi.astype(F32)
        mid = r1.astype(BF16)
        lo = (r1 - mid.astype(F32)).astype(BF16)
        cum = _dot(tri_bf, hi) + _dot(tri_bf, mid) + _dot(tri_bf, lo)
        cl = cum[c - 1:c, :]
        qf = q_ref[rows, :].astype(F32) * (GLA_DK ** -0.5)
        kf = k_ref[rows, :].astype(F32)
        q_dec = qf * jnp.exp(cum)
        k_inv = (kf * jnp.exp(-cum)).astype(BF16)
        k_end = kf * jnp.exp(cl - cum)
        st_bf = st.astype(BF16)
        upd = jnp.zeros_like(st)
        for hh in range(2):
            hm = (lane >= hh * GLA_DK) & (lane < (hh + 1) * GLA_DK)
            vcols = slice(hh * GLA_DV, (hh + 1) * GLA_DV)
            qd = jnp.where(hm, q_dec, 0.0).astype(BF16)
            att = jnp.where(tril, _dot_nt(qd, k_inv), 0.0).astype(BF16)
            vh = v_ref[rows, vcols]
            o = _dot(att, vh) + _dot_nt(qd, st_bf)
            o = _rms(o, gn_ref[:, vcols])
            rr = r_ref[rows, vcols].astype(F32)
            o_ref[rows, vcols] = (o * (rr * _sigmoid(rr))).astype(BF16)
            ke = jnp.where(hm, k_end, 0.0).astype(BF16)
            upd = upd + _dot_tn(vh, ke)
        return st * jnp.exp(cl) + upd

    lax.fori_loop(0, s_len // c, body, jnp.zeros((GLA_DV, 2 * GLA_DK), F32))


def _gla(q, k, v, r, misc, wa2, ba, gn, batch, seq):
    t = q.shape[0]
    kw = 2 * GLA_DK
    vw = 2 * GLA_DV
    return pl.pallas_call(
        _gla_kernel,
        grid=(batch, GLA_HEADS // 2),
        in_specs=[
            pl.BlockSpec((seq, kw), lambda b, p: (b, p)),
            pl.BlockSpec((seq, kw), lambda b, p: (b, p)),
            pl.BlockSpec((seq, vw), lambda b, p: (b, p)),
            pl.BlockSpec((seq, vw), lambda b, p: (b, p)),
            pl.BlockSpec((seq, LANES), lambda b, p: (b, 0)),
            pl.BlockSpec((LANES, kw), lambda b, p: (0, p)),
            pl.BlockSpec((1, kw), lambda b, p: (0, p)),
            pl.BlockSpec((1, vw), lambda b, p: (0, p)),
        ],
        out_specs=pl.BlockSpec((seq, vw), lambda b, p: (b, p)),
        out_shape=jax.ShapeDtypeStruct((t, GLA_VAL_WIDTH), BF16),
        compiler_params=_params(2),
        name="gla",
    )(q, k, v, r, misc, wa2, ba, gn)


def _mla_prep_kernel(cq_ref, ckv_ref, misc_ref, c_ref, s_ref, qn_ref, kvn_ref,
                     wq_ref, wk_ref, wv_ref, q_out, k_out, v_out):
    c_tab = c_ref[...]
    s_tab = s_ref[...]
    scale = MLA_QK ** -0.5
    cqn = _rms(cq_ref[...].astype(F32), qn_ref[...]).astype(BF16)
    qf = _dot(cqn, wq_ref[...])
    ckvn = _rms(ckv_ref[...].astype(F32), kvn_ref[...]).astype(BF16)
    kf = _dot(ckvn, wk_ref[...])
    v_out[...] = _dot(ckvn, wv_ref[...]).astype(BF16)
    lane = lax.broadcasted_iota(jnp.int32, (1, HEAD_PAD), 1)
    rope_lane = ((lane >= 32) & (lane < 48)) | ((lane >= 96) & (lane < 112))
    kr = misc_ref[...].astype(F32)
    kr = jnp.where(rope_lane, kr * c_tab + pltpu.roll(kr, 64, axis=1) * s_tab, 0.0)
    for h in range(MLA_HEADS):
        cols = slice(h * HEAD_PAD, (h + 1) * HEAD_PAD)
        x = qf[:, cols]
        q_out[:, cols] = ((x * c_tab + pltpu.roll(x, 64, axis=1) * s_tab) * scale).astype(BF16)
        k_out[:, cols] = (kf[:, cols] + kr).astype(BF16)


def _mla_prep(cq, ckv, misc, c_tab, s_tab, qn, kvn, wq, wk, wv, tm=512):
    t = cq.shape[0]
    hw = MLA_HEADS * HEAD_PAD
    row = lambda w: pl.BlockSpec((tm, w), lambda i: (i, 0))
    return pl.pallas_call(
        _mla_prep_kernel,
        grid=(t // tm,),
        in_specs=[
            row(MLA_Q_RANK), row(MLA_KV_RANK), row(LANES), row(HEAD_PAD), row(HEAD_PAD),
            _const_spec((1, MLA_Q_RANK)), _const_spec((1, MLA_KV_RANK)),
            _const_spec((MLA_Q_RANK, hw)), _const_spec((MLA_KV_RANK, hw)),
            _const_spec((MLA_KV_RANK, hw)),
        ],
        out_specs=(row(hw), row(hw), row(hw)),
        out_shape=(jax.ShapeDtypeStruct((t, hw), BF16),) * 3,
        compiler_params=_params(1),
        name="mla_prep",
    )(cq, ckv, misc, c_tab, s_tab, qn, kvn, wq, wk, wv)


ATT_TQ = 512
ATT_TK = 256


def _attn_kernel(q_ref, k_ref, v_ref, o_ref):
    qi = pl.program_id(2)
    tq, tk = ATT_TQ, ATT_TK
    neg = jnp.finfo(F32).min
    qpos = qi * tq + lax.broadcasted_iota(jnp.int32, (tq, tk), 0)
    kidx = lax.broadcasted_iota(jnp.int32, (tq, tk), 1)
    out = None
    for hh in range(2):
        cols = slice(hh * HEAD_PAD, (hh + 1) * HEAD_PAD)
        q = q_ref[:, cols]

        def step(j, carry, masked):
            m, l, acc = carry
            rows = pl.ds(pl.multiple_of(j * tk, tk), tk)
            s = _dot_nt(q, k_ref[rows, cols])
            if masked:
                s = jnp.where(qpos >= j * tk + kidx, s, neg)
            m_new = jnp.maximum(m, jnp.max(s, axis=-1, keepdims=True))
            alpha = jnp.exp(m - m_new)
            p = jnp.exp(s - m_new)
            l = alpha * l + jnp.sum(p, axis=-1, keepdims=True)
            acc = alpha * acc + _dot(p.astype(BF16), v_ref[rows, cols])
            return m_new, l, acc

        init = (jnp.full((tq, 1), neg, F32), jnp.zeros((tq, 1), F32),
                jnp.zeros((tq, HEAD_PAD), F32))
        n_full = qi * (tq // tk)
        carry = lax.fori_loop(0, n_full, functools.partial(step, masked=False), init)
        for d in range(tq // tk):
            carry = step(n_full + d, carry, True)
        _, l, acc = carry
        o = acc / l
        out = o if out is None else out + o
    o_ref[...] = out.astype(BF16)


def _attention(q, k, v, batch, seq):
    t = q.shape[0]
    nq = seq // ATT_TQ
    pw = 2 * HEAD_PAD
    return pl.pallas_call(
        _attn_kernel,
        grid=(batch, MLA_HEADS // 2, nq),
        in_specs=[
            pl.BlockSpec((ATT_TQ, pw), lambda b, p, i: (b * nq + i, p)),
            pl.BlockSpec((seq, pw), lambda b, p, i: (b, p)),
            pl.BlockSpec((seq, pw), lambda b, p, i: (b, p)),
        ],
        out_specs=pl.BlockSpec((ATT_TQ, HEAD_PAD), lambda b, p, i: (b * nq + i, p)),
        out_shape=jax.ShapeDtypeStruct((t, MLA_VAL_WIDTH), BF16),
        compiler_params=_params(3),
        name="attention",
    )(q, k, v)


def _merge_kernel(x_ref, pa_ref, gb_ref, mc_ref, gl_ref, wa_ref, wb_ref, wc_ref, wo_ref,
                  n_ref, o_ref):
    merged = None
    for i, (b_ref, w_ref) in enumerate(((pa_ref, wa_ref), (gb_ref, wb_ref), (mc_ref, wc_ref))):
        y = _dot(b_ref[...], w_ref[...])
        gate = _sigmoid(gl_ref[:, i * D_MODEL:(i + 1) * D_MODEL].astype(F32))
        merged = gate * y if merged is None else merged + gate * y
    z = _dot(merged.astype(BF16), wo_ref[...])
    o_ref[...] = x_ref[...] + _rms(z, n_ref[...])


def _merge(x2, pa, gb, mc, glog, wa, wb, wc, wo, n_post, tm=512):
    t = x2.shape[0]
    row = lambda w: pl.BlockSpec((tm, w), lambda i: (i, 0))
    return pl.pallas_call(
        _merge_kernel,
        grid=(t // tm,),
        in_specs=[
            row(D_MODEL), row(POOL_WIDTH), row(GLA_VAL_WIDTH), row(MLA_VAL_WIDTH),
            row(N_BRANCH * D_MODEL),
            _const_spec((POOL_WIDTH, D_MODEL)), _const_spec((GLA_VAL_WIDTH, D_MODEL)),
            _const_spec((MLA_VAL_WIDTH, D_MODEL)), _const_spec((D_MODEL, D_MODEL)),
            _const_spec((1, D_MODEL)),
        ],
        out_specs=row(D_MODEL),
        out_shape=jax.ShapeDtypeStruct((t, D_MODEL), F32),
        compiler_params=_params(1),
        name="merge",
    )(x2, pa, gb, mc, glog, wa, wb, wc, wo, n_post)


FFN_CHUNKS = ((0, 1024), (1024, 2048), (2048, D_FF))


def _ffn_kernel(x_ref, n1_ref, wg_ref, wu_ref, wd_ref, n2_ref, o_ref):
    x = x_ref[...]
    h = _rms(x, n1_ref[...]).astype(BF16)
    acc = None
    for a, b in FFN_CHUNKS:
        g = _dot(h, wg_ref[:, a:b])
        u = _dot(h, wu_ref[:, a:b])
        act = (g * _sigmoid(g) * u).astype(BF16)
        d = _dot(act, wd_ref[a:b, :])
        acc = d if acc is None else acc + d
    o_ref[...] = x + _rms(acc, n2_ref[...])


def _ffn(x2, n1, wg, wu, wd, n2, tm=512):
    t = x2.shape[0]
    row = pl.BlockSpec((tm, D_MODEL), lambda i: (i, 0))
    return pl.pallas_call(
        _ffn_kernel,
        grid=(t // tm,),
        in_specs=[
            row, _const_spec((1, D_MODEL)),
            _const_spec((D_MODEL, D_FF)), _const_spec((D_MODEL, D_FF)),
            _const_spec((D_FF, D_MODEL)), _const_spec((1, D_MODEL)),
        ],
        out_specs=row,
        out_shape=jax.ShapeDtypeStruct((t, D_MODEL), F32),
        compiler_params=_params(1),
        name="ffn",
    )(x2, n1, wg, wu, wd, n2)


def _prep_weights(w_in, w_gla_a2, w_mla_uq, w_mla_ukv):
    sizes = (POOL_WIDTH, GLA_KEY_WIDTH, GLA_KEY_WIDTH, GLA_VAL_WIDTH, GLA_VAL_WIDTH,
             GLA_GATE_RANK, MLA_Q_RANK, MLA_KV_RANK, MLA_ROPE, N_BRANCH * D_MODEL)
    offs = [0]
    for s in sizes:
        offs.append(offs[-1] + s)
    seg = [w_in[:, :, offs[i]:offs[i + 1]] for i in range(len(sizes))]
    w_pool_c, w_q, w_k, w_v, w_r, w_a1, w_cq, w_ckv, w_kr, w_gate = seg
    d = w_in.shape[0]
    z16 = jnp.zeros((d, D_MODEL, 16), w_in.dtype)
    z48 = jnp.zeros((d, D_MODEL, 48), w_in.dtype)
    w_misc = jnp.concatenate([w_a1, z16, w_kr[..., :16], z48, w_kr[..., 16:], z16], axis=-1)
    w_in_r = jnp.concatenate(
        [w_pool_c, w_q, w_k, w_v, w_r, w_cq, w_ckv, w_misc, w_gate], axis=-1).astype(BF16)

    wa2 = jnp.pad(w_gla_a2, ((0, 0), (0, LANES - GLA_GATE_RANK), (0, 0))).astype(BF16)

    uq = w_mla_uq.reshape(d, MLA_Q_RANK, MLA_HEADS, MLA_QK)
    zq = jnp.zeros((d, MLA_Q_RANK, MLA_HEADS, 16), w_mla_uq.dtype)
    wq = jnp.concatenate(
        [uq[..., 0:32], uq[..., 64:80], zq, uq[..., 32:64], uq[..., 80:96], zq],
        axis=-1).reshape(d, MLA_Q_RANK, MLA_HEADS * HEAD_PAD).astype(BF16)

    ukv = w_mla_ukv.reshape(d, MLA_KV_RANK, MLA_HEADS, MLA_NOPE + MLA_V)
    zk = jnp.zeros((d, MLA_KV_RANK, MLA_HEADS, 32), w_mla_ukv.dtype)
    wk = jnp.concatenate(
        [ukv[..., 0:32], zk, ukv[..., 32:64], zk],
        axis=-1).reshape(d, MLA_KV_RANK, MLA_HEADS * HEAD_PAD).astype(BF16)
    vv = ukv[..., MLA_NOPE:].reshape(d, MLA_KV_RANK, MLA_HEADS // 2, 2, MLA_V)
    zv = jnp.zeros((d, MLA_KV_RANK, MLA_HEADS // 2, MLA_V), w_mla_ukv.dtype)
    wv = jnp.concatenate(
        [vv[:, :, :, 0], zv, zv, vv[:, :, :, 1]],
        axis=-1).reshape(d, MLA_KV_RANK, MLA_HEADS * HEAD_PAD).astype(BF16)
    return w_in_r, wa2, wq, wk, wv


def kernel(x, positions, norm_pre_mix, norm_post_mix, norm_pre_ffn, norm_post_ffn, w_in, w_pool,
           pool_scale, w_a, w_gla_a2, b_gla_a, gla_norm, w_b, mla_q_norm, w_mla_uq, mla_kv_norm,
           w_mla_ukv, w_c, w_o, w_ffn_gu, w_ffn_down):
    batch, seq, d_model = x.shape
    t = batch * seq
    x2 = x.reshape(t, d_model)
    c_tab, s_tab = _rope_tables(positions)
    w_in_r, wa2, wq, wk, wv = _prep_weights(w_in, w_gla_a2, w_mla_uq, w_mla_ukv)
    w_pool_b = w_pool.astype(BF16)
    w_a_b, w_b_b, w_c_b, w_o_b = (w.astype(BF16) for w in (w_a, w_b, w_c, w_o))
    w_g_b = w_ffn_gu[:, :, :D_FF].astype(BF16)
    w_u_b = w_ffn_gu[:, :, D_FF:].astype(BF16)
    w_d_b = w_ffn_down.astype(BF16)
    row = lambda a, l: a[l].reshape(1, -1)
    for l in range(DEPTH):
        u_pool, gq, gk, gv, gr, cq, ckv, misc, glog = _in_proj(x2, row(norm_pre_mix, l), w_in_r[l])
        pa = _pool(u_pool, w_pool_b[l], row(pool_scale, l), batch, seq)
        gb = _gla(gq, gk, gv, gr, misc, wa2[l], row(b_gla_a, l), row(gla_norm, l), batch, seq)
        mq, mk, mv = _mla_prep(cq, ckv, misc, c_tab, s_tab, row(mla_q_norm, l),
                               row(mla_kv_norm, l), wq[l], wk[l], wv[l])
        mc = _attention(mq, mk, mv, batch, seq)
        x2 = _merge(x2, pa, gb, mc, glog, w_a_b[l], w_b_b[l], w_c_b[l], w_o_b[l],
                    row(norm_post_mix, l))
        x2 = _ffn(x2, row(norm_pre_ffn, l), w_g_b[l], w_u_b[l], w_d_b[l], row(norm_post_ffn, l))
    return x2.reshape(batch, seq, d_model)
```

```python
import functools

import jax
import jax.numpy as jnp
from jax import lax
from jax.experimental import pallas as pl
from jax.experimental.pallas import tpu as pltpu

F32 = jnp.float32
BF16 = jnp.bfloat16

D_MODEL = 1024
DEPTH = 4
POOL_WIDTH = 512
POOL_WINDOWS = (2, 4, 8, 16)
POOL_GROUP_DIM = 128
GLA_HEADS = 4
GLA_DK = 64
GLA_DV = 128
GLA_KEY_WIDTH = GLA_HEADS * GLA_DK
GLA_VAL_WIDTH = GLA_HEADS * GLA_DV
GLA_GATE_RANK = 16
GLA_GATE_TAU = 16.0
GLA_CHUNK = 64
GLA_SUPER = 256
MLA_HEADS = 8
MLA_Q_RANK = 384
MLA_KV_RANK = 256
MLA_NOPE = 64
MLA_ROPE = 32
MLA_V = 64
MLA_QK = MLA_NOPE + MLA_ROPE
MLA_VAL_WIDTH = MLA_HEADS * MLA_V
ROPE_BASE = 10000.0
N_BRANCH = 3
D_FF = 2816
EPS = 1e-6

LANES = 128
HEAD_PAD = 128
VMEM_LIMIT = 56 * 1024 * 1024

IN_SEGS = (
    ("pool", POOL_WIDTH, F32),
    ("q", GLA_KEY_WIDTH, BF16),
    ("k", GLA_KEY_WIDTH, BF16),
    ("v", GLA_VAL_WIDTH, BF16),
    ("r", GLA_VAL_WIDTH, BF16),
    ("cq", MLA_Q_RANK, BF16),
    ("ckv", MLA_KV_RANK, BF16),
    ("misc", LANES, BF16),
    ("glog", N_BRANCH * D_MODEL, BF16),
)
IN_WIDTH_R = sum(s[1] for s in IN_SEGS)


def _dot(a, b):
    return jnp.dot(a, b, preferred_element_type=F32)


def _dot_nt(a, b):
    return lax.dot_general(a, b, (((1,), (1,)), ((), ())), preferred_element_type=F32)


def _dot_tn(a, b):
    return lax.dot_general(a, b, (((0,), (0,)), ((), ())), preferred_element_type=F32)


def _rms(x, g):
    return x * lax.rsqrt(jnp.mean(x * x, axis=-1, keepdims=True) + EPS) * g


def _sigmoid(x):
    return 0.5 * jnp.tanh(0.5 * x) + 0.5


def _params(n_axes):
    return pltpu.CompilerParams(
        dimension_semantics=("arbitrary",) * n_axes, vmem_limit_bytes=VMEM_LIMIT)


def _const_spec(shape):
    nd = len(shape)
    return pl.BlockSpec(shape, lambda *_: (0,) * nd, pipeline_mode=pl.Buffered(1))


def _rope_kernel(pos_ref, invf_ref, cos_ref, sin_ref):
    ang = pos_ref[...].astype(F32) * invf_ref[...]
    cos_ref[...] = jnp.cos(ang)
    sin_ref[...] = jnp.sin(ang)


def _rope_tables(positions):
    t = positions.size
    half = MLA_ROPE // 2
    inv_freq = ROPE_BASE ** (-jnp.arange(0, MLA_ROPE, 2, dtype=F32) / MLA_ROPE)
    rows = t * half // LANES
    pos_d = jnp.repeat(positions.reshape(t), half).reshape(rows, LANES)
    invf_d = jnp.tile(inv_freq, LANES // half).reshape(1, LANES)
    cos_d, sin_d = pl.pallas_call(
        _rope_kernel,
        out_shape=(jax.ShapeDtypeStruct((rows, LANES), F32),) * 2,
        name="rope_tables",
    )(pos_d, invf_d)
    cos = cos_d.reshape(t, half)
    sin = sin_d.reshape(t, half)
    one = jnp.ones((t, 32), F32)
    zero = jnp.zeros((t, half), F32)
    zero32 = jnp.zeros((t, 32), F32)
    c_tab = jnp.concatenate([one, cos, zero, one, cos, zero], axis=1)
    s_tab = jnp.concatenate([zero32, -sin, zero, zero32, sin, zero], axis=1)
    return c_tab, s_tab


def _in_proj_kernel(x_ref, g_ref, w_ref, *out_refs):
    h = _rms(x_ref[...], g_ref[...]).astype(BF16)
    c0 = 0
    for out_ref, (_, width, dtype) in zip(out_refs, IN_SEGS):
        for a in range(0, width, 1024):
            b = min(a + 1024, width)
            out_ref[:, a:b] = _dot(h, w_ref[:, c0 + a:c0 + b]).astype(dtype)
        c0 += width


def _in_proj(x2, g, w, tm=512):
    t = x2.shape[0]
    out_shape = tuple(jax.ShapeDtypeStruct((t, wd), dt) for _, wd, dt in IN_SEGS)
    out_specs = tuple(pl.BlockSpec((tm, wd), lambda i: (i, 0)) for _, wd, _ in IN_SEGS)
    return pl.pallas_call(
        _in_proj_kernel,
        grid=(t // tm,),
        in_specs=[
            pl.BlockSpec((tm, D_MODEL), lambda i: (i, 0)),
            _const_spec((1, D_MODEL)),
            _const_spec((D_MODEL, IN_WIDTH_R)),
        ],
        out_specs=out_specs,
        out_shape=out_shape,
        compiler_params=_params(1),
        name="in_proj",
    )(x2, g, w)


def _pool_kernel(u_ref, wp_ref, ps_ref, o_ref):
    s_len = u_ref.shape[0]
    t = lax.broadcasted_iota(jnp.int32, (s_len, POOL_GROUP_DIM), 0)
    for g, w in enumerate(POOL_WINDOWS):
        cols = slice(g * POOL_GROUP_DIM, (g + 1) * POOL_GROUP_DIM)
        u = u_ref[:, cols]
        s = u
        sh = 1
        while sh < w:
            s = s + jnp.where(t >= sh, pltpu.roll(s, sh, axis=0), 0.0)
            sh *= 2
        cnt = jnp.minimum(t + 1, w).astype(F32)
        diff = s / cnt - u
        y = _dot(diff.astype(BF16), wp_ref[g]) * ps_ref[:, cols]
        o_ref[:, cols] = y.astype(BF16)


def _pool(u, wp, ps, batch, seq):
    t = u.shape[0]
    return pl.pallas_call(
        _pool_kernel,
        grid=(batch,),
        in_specs=[
            pl.BlockSpec((seq, POOL_WIDTH), lambda b: (b, 0)),
            _const_spec(wp.shape),
            _const_spec((1, POOL_WIDTH)),
        ],
        out_specs=pl.BlockSpec((seq, POOL_WIDTH), lambda b: (b, 0)),
        out_shape=jax.ShapeDtypeStruct((t, POOL_WIDTH), BF16),
        compiler_params=_params(1),
        name="pool",
    )(u, wp, ps)


def _gla_kernel(q_ref, k_ref, v_ref, r_ref, misc_ref, wa2_ref, ba_ref, gn_ref, o_ref):
    s_len = q_ref.shape[0]
    c = GLA_CHUNK
    sc = GLA_SUPER
    row = lax.broadcasted_iota(jnp.int32, (sc, sc), 0)
    col = lax.broadcasted_iota(jnp.int32, (sc, sc), 1)
    same_chunk = (row // c) == (col // c)
    tril = same_chunk & (row >= col)
    tri_bf = tril.astype(BF16)
    ones_bf = same_chunk.astype(BF16)
    lane = lax.broadcasted_iota(jnp.int32, (1, 2 * GLA_DK), 1)
    head_lanes = [(lane >= hh * GLA_DK) & (lane < (hh + 1) * GLA_DK) for hh in range(2)]
    vcols = [slice(hh * GLA_DV, (hh + 1) * GLA_DV) for hh in range(2)]

    def body(n, st):
        rows = pl.ds(pl.multiple_of(n * sc, sc), sc)
        z = _dot(misc_ref[rows, :], wa2_ref[...]) + ba_ref[...]
        la = (jnp.minimum(z, 0.0) - jnp.log1p(jnp.exp(-jnp.abs(z)))) * (1.0 / GLA_GATE_TAU)
        hi = la.astype(BF16)
        r1 = la - hi.astype(F32)
        mid = r1.astype(BF16)
        lo = (r1 - mid.astype(F32)).astype(BF16)
        cum = _dot(tri_bf, hi) + _dot(tri_bf, mid) + _dot(tri_bf, lo)
        tot = _dot(ones_bf, hi) + _dot(ones_bf, mid) + _dot(ones_bf, lo)
        qf = q_ref[rows, :].astype(F32) * (GLA_DK ** -0.5)
        kf = k_ref[rows, :].astype(F32)
        q_dec = qf * jnp.exp(cum)
        k_inv = (kf * jnp.exp(-cum)).astype(BF16)
        k_end = kf * jnp.exp(tot - cum)
        dec = jnp.exp(tot)
        vh = [v_ref[rows, vcols[hh]] for hh in range(2)]
        ke = [jnp.where(head_lanes[hh], k_end, 0.0).astype(BF16) for hh in range(2)]
        sts = []
        for j in range(sc // c):
            sts.append(st.astype(BF16))
            rs = slice(j * c, (j + 1) * c)
            upd = _dot_tn(vh[0][rs], ke[0][rs]) + _dot_tn(vh[1][rs], ke[1][rs])
            st = st * dec[j * c:j * c + 1, :] + upd
        for hh in range(2):
            qd = jnp.where(head_lanes[hh], q_dec, 0.0).astype(BF16)
            att = jnp.where(tril, _dot_nt(qd, k_inv), 0.0).astype(BF16)
            o_inter = jnp.concatenate(
                [_dot_nt(qd[j * c:(j + 1) * c], sts[j]) for j in range(sc // c)], axis=0)
            o = _rms(_dot(att, vh[hh]) + o_inter, gn_ref[:, vcols[hh]])
            rr = r_ref[rows, vcols[hh]].astype(F32)
            o_ref[rows, vcols[hh]] = (o * (rr * _sigmoid(rr))).astype(BF16)
        return st

    lax.fori_loop(0, s_len // sc, body, jnp.zeros((GLA_DV, 2 * GLA_DK), F32))


def _gla(q, k, v, r, misc, wa2, ba, gn, batch, seq):
    t = q.shape[0]
    kw = 2 * GLA_DK
    vw = 2 * GLA_DV
    return pl.pallas_call(
        _gla_kernel,
        grid=(batch, GLA_HEADS // 2),
        in_specs=[
            pl.BlockSpec((seq, kw), lambda b, p: (b, p)),
            pl.BlockSpec((seq, kw), lambda b, p: (b, p)),
            pl.BlockSpec((seq, vw), lambda b, p: (b, p)),
            pl.BlockSpec((seq, vw), lambda b, p: (b, p)),
            pl.BlockSpec((seq, LANES), lambda b, p: (b, 0)),
            pl.BlockSpec((LANES, kw), lambda b, p: (0, p)),
            pl.BlockSpec((1, kw), lambda b, p: (0, p)),
            pl.BlockSpec((1, vw), lambda b, p: (0, p)),
        ],
        out_specs=pl.BlockSpec((seq, vw), lambda b, p: (b, p)),
        out_shape=jax.ShapeDtypeStruct((t, GLA_VAL_WIDTH), BF16),
        compiler_params=_params(2),
        name="gla",
    )(q, k, v, r, misc, wa2, ba, gn)


def _mla_prep_kernel(cq_ref, ckv_ref, misc_ref, c_ref, s_ref, qn_ref, kvn_ref,
                     wq_ref, wk_ref, wv_ref, q_out, k_out, v_out):
    c_tab = c_ref[...]
    s_tab = s_ref[...]
    scale = MLA_QK ** -0.5
    cqn = _rms(cq_ref[...].astype(F32), qn_ref[...]).astype(BF16)
    qf = _dot(cqn, wq_ref[...])
    ckvn = _rms(ckv_ref[...].astype(F32), kvn_ref[...]).astype(BF16)
    kf = _dot(ckvn, wk_ref[...])
    v_out[...] = _dot(ckvn, wv_ref[...]).astype(BF16)
    lane = lax.broadcasted_iota(jnp.int32, (1, HEAD_PAD), 1)
    rope_lane = ((lane >= 32) & (lane < 48)) | ((lane >= 96) & (lane < 112))
    kr = misc_ref[...].astype(F32)
    kr = jnp.where(rope_lane, kr * c_tab + pltpu.roll(kr, 64, axis=1) * s_tab, 0.0)
    for h in range(MLA_HEADS):
        cols = slice(h * HEAD_PAD, (h + 1) * HEAD_PAD)
        x = qf[:, cols]
        q_out[:, cols] = ((x * c_tab + pltpu.roll(x, 64, axis=1) * s_tab) * scale).astype(BF16)
        k_out[:, cols] = (kf[:, cols] + kr).astype(BF16)


def _mla_prep(cq, ckv, misc, c_tab, s_tab, qn, kvn, wq, wk, wv, tm=512):
    t = cq.shape[0]
    hw = MLA_HEADS * HEAD_PAD
    row = lambda w: pl.BlockSpec((tm, w), lambda i: (i, 0))
    return pl.pallas_call(
        _mla_prep_kernel,
        grid=(t // tm,),
        in_specs=[
            row(MLA_Q_RANK), row(MLA_KV_RANK), row(LANES), row(HEAD_PAD), row(HEAD_PAD),
            _const_spec((1, MLA_Q_RANK)), _const_spec((1, MLA_KV_RANK)),
            _const_spec((MLA_Q_RANK, hw)), _const_spec((MLA_KV_RANK, hw)),
            _const_spec((MLA_KV_RANK, hw)),
        ],
        out_specs=(row(hw), row(hw), row(hw)),
        out_shape=(jax.ShapeDtypeStruct((t, hw), BF16),) * 3,
        compiler_params=_params(1),
        name="mla_prep",
    )(cq, ckv, misc, c_tab, s_tab, qn, kvn, wq, wk, wv)


ATT_TQ = 512
ATT_HALF = ATT_TQ // 2


def _attn_block(q_ref, k_ref, v_ref, qi, cols):
    tq, hf = ATT_TQ, ATT_HALF
    r0 = qi * tq
    neg = jnp.finfo(F32).min
    q = q_ref[r0:r0 + tq, cols]
    k_d = k_ref[r0:r0 + tq, cols]
    v_d = v_ref[r0:r0 + tq, cols]
    row0 = lax.broadcasted_iota(jnp.int32, (hf, hf), 0)
    col0 = lax.broadcasted_iota(jnp.int32, (hf, hf), 1)
    row1 = lax.broadcasted_iota(jnp.int32, (hf, tq), 0)
    col1 = lax.broadcasted_iota(jnp.int32, (hf, tq), 1)
    d0 = jnp.where(col0 <= row0, _dot_nt(q[:hf], k_d[:hf]), neg)
    d1 = jnp.where(col1 <= row1 + hf, _dot_nt(q[hf:], k_d), neg)
    m0 = jnp.max(d0, axis=-1, keepdims=True)
    m1 = jnp.max(d1, axis=-1, keepdims=True)
    if qi > 0:
        s_off = _dot_nt(q, k_ref[0:r0, cols])
        m0 = jnp.maximum(m0, jnp.max(s_off[:hf], axis=-1, keepdims=True))
        m1 = jnp.maximum(m1, jnp.max(s_off[hf:], axis=-1, keepdims=True))
        p_off0 = jnp.exp(s_off[:hf] - m0)
        p_off1 = jnp.exp(s_off[hf:] - m1)
        p_off = jnp.concatenate([p_off0.astype(BF16), p_off1.astype(BF16)], axis=0)
        o_off = _dot(p_off, v_ref[0:r0, cols])
    p0 = jnp.exp(d0 - m0)
    p1 = jnp.exp(d1 - m1)
    l0 = jnp.sum(p0, axis=-1, keepdims=True)
    l1 = jnp.sum(p1, axis=-1, keepdims=True)
    o0 = _dot(p0.astype(BF16), v_d[:hf])
    o1 = _dot(p1.astype(BF16), v_d)
    if qi > 0:
        l0 = l0 + jnp.sum(p_off0, axis=-1, keepdims=True)
        l1 = l1 + jnp.sum(p_off1, axis=-1, keepdims=True)
        o0 = o0 + o_off[:hf]
        o1 = o1 + o_off[hf:]
    return o0 / l0, o1 / l1


def _attn_kernel(q_ref, k_ref, v_ref, o_ref):
    tq, hf = ATT_TQ, ATT_HALF
    for qi in range(q_ref.shape[0] // tq):
        a0, a1 = _attn_block(q_ref, k_ref, v_ref, qi, slice(0, HEAD_PAD))
        b0, b1 = _attn_block(q_ref, k_ref, v_ref, qi, slice(HEAD_PAD, 2 * HEAD_PAD))
        o_ref[qi * tq:qi * tq + hf, :] = (a0 + b0).astype(BF16)
        o_ref[qi * tq + hf:(qi + 1) * tq, :] = (a1 + b1).astype(BF16)


def _attention(q, k, v, batch, seq):
    t = q.shape[0]
    pw = 2 * HEAD_PAD
    pair = pl.BlockSpec((seq, pw), lambda b, p: (b, p))
    return pl.pallas_call(
        _attn_kernel,
        grid=(batch, MLA_HEADS // 2),
        in_specs=[pair, pair, pair],
        out_specs=pl.BlockSpec((seq, HEAD_PAD), lambda b, p: (b, p)),
        out_shape=jax.ShapeDtypeStruct((t, MLA_VAL_WIDTH), BF16),
        compiler_params=_params(2),
        name="attention",
    )(q, k, v)


def _merge_kernel(x_ref, pa_ref, gb_ref, mc_ref, gl_ref, wa_ref, wb_ref, wc_ref, wo_ref,
                  n_ref, o_ref):
    merged = None
    for i, (b_ref, w_ref) in enumerate(((pa_ref, wa_ref), (gb_ref, wb_ref), (mc_ref, wc_ref))):
        y = _dot(b_ref[...], w_ref[...])
        gate = _sigmoid(gl_ref[:, i * D_MODEL:(i + 1) * D_MODEL].astype(F32))
        merged = gate * y if merged is None else merged + gate * y
    z = _dot(merged.astype(BF16), wo_ref[...])
    o_ref[...] = x_ref[...] + _rms(z, n_ref[...])


def _merge(x2, pa, gb, mc, glog, wa, wb, wc, wo, n_post, tm=512):
    t = x2.shape[0]
    row = lambda w: pl.BlockSpec((tm, w), lambda i: (i, 0))
    return pl.pallas_call(
        _merge_kernel,
        grid=(t // tm,),
        in_specs=[
            row(D_MODEL), row(POOL_WIDTH), row(GLA_VAL_WIDTH), row(MLA_VAL_WIDTH),
            row(N_BRANCH * D_MODEL),
            _const_spec((POOL_WIDTH, D_MODEL)), _const_spec((GLA_VAL_WIDTH, D_MODEL)),
            _const_spec((MLA_VAL_WIDTH, D_MODEL)), _const_spec((D_MODEL, D_MODEL)),
            _const_spec((1, D_MODEL)),
        ],
        out_specs=row(D_MODEL),
        out_shape=jax.ShapeDtypeStruct((t, D_MODEL), F32),
        compiler_params=_params(1),
        name="merge",
    )(x2, pa, gb, mc, glog, wa, wb, wc, wo, n_post)


FFN_CHUNKS = ((0, 1024), (1024, 2048), (2048, D_FF))


def _ffn_kernel(x_ref, n1_ref, wg_ref, wu_ref, wd_ref, n2_ref, o_ref):
    x = x_ref[...]
    h = _rms(x, n1_ref[...]).astype(BF16)
    acc = None
    for a, b in FFN_CHUNKS:
        g = _dot(h, wg_ref[:, a:b])
        u = _dot(h, wu_ref[:, a:b])
        act = (g * _sigmoid(g) * u).astype(BF16)
        d = _dot(act, wd_ref[a:b, :])
        acc = d if acc is None else acc + d
    o_ref[...] = x + _rms(acc, n2_ref[...])


def _ffn(x2, n1, wg, wu, wd, n2, tm=512):
    t = x2.shape[0]
    row = pl.BlockSpec((tm, D_MODEL), lambda i: (i, 0))
    return pl.pallas_call(
        _ffn_kernel,
        grid=(t // tm,),
        in_specs=[
            row, _const_spec((1, D_MODEL)),
            _const_spec((D_MODEL, D_FF)), _const_spec((D_MODEL, D_FF)),
            _const_spec((D_FF, D_MODEL)), _const_spec((1, D_MODEL)),
        ],
        out_specs=row,
        out_shape=jax.ShapeDtypeStruct((t, D_MODEL), F32),
        compiler_params=_params(1),
        name="ffn",
    )(x2, n1, wg, wu, wd, n2)


def _prep_weights(w_in, w_gla_a2, w_mla_uq, w_mla_ukv):
    sizes = (POOL_WIDTH, GLA_KEY_WIDTH, GLA_KEY_WIDTH, GLA_VAL_WIDTH, GLA_VAL_WIDTH,
             GLA_GATE_RANK, MLA_Q_RANK, MLA_KV_RANK, MLA_ROPE, N_BRANCH * D_MODEL)
    offs = [0]
    for s in sizes:
        offs.append(offs[-1] + s)
    seg = [w_in[:, :, offs[i]:offs[i + 1]] for i in range(len(sizes))]
    w_pool_c, w_q, w_k, w_v, w_r, w_a1, w_cq, w_ckv, w_kr, w_gate = seg
    d = w_in.shape[0]
    z16 = jnp.zeros((d, D_MODEL, 16), w_in.dtype)
    z48 = jnp.zeros((d, D_MODEL, 48), w_in.dtype)
    w_misc = jnp.concatenate([w_a1, z16, w_kr[..., :16], z48, w_kr[..., 16:], z16], axis=-1)
    w_in_r = jnp.concatenate(
        [w_pool_c, w_q, w_k, w_v, w_r, w_cq, w_ckv, w_misc, w_gate], axis=-1).astype(BF16)

    wa2 = jnp.pad(w_gla_a2, ((0, 0), (0, LANES - GLA_GATE_RANK), (0, 0))).astype(BF16)

    uq = w_mla_uq.reshape(d, MLA_Q_RANK, MLA_HEADS, MLA_QK)
    zq = jnp.zeros((d, MLA_Q_RANK, MLA_HEADS, 16), w_mla_uq.dtype)
    wq = jnp.concatenate(
        [uq[..., 0:32], uq[..., 64:80], zq, uq[..., 32:64], uq[..., 80:96], zq],
        axis=-1).reshape(d, MLA_Q_RANK, MLA_HEADS * HEAD_PAD).astype(BF16)

    ukv = w_mla_ukv.reshape(d, MLA_KV_RANK, MLA_HEADS, MLA_NOPE + MLA_V)
    zk = jnp.zeros((d, MLA_KV_RANK, MLA_HEADS, 32), w_mla_ukv.dtype)
    wk = jnp.concatenate(
        [ukv[..., 0:32], zk, ukv[..., 32:64], zk],
        axis=-1).reshape(d, MLA_KV_RANK, MLA_HEADS * HEAD_PAD).astype(BF16)
    vv = ukv[..., MLA_NOPE:].reshape(d, MLA_KV_RANK, MLA_HEADS // 2, 2, MLA_V)
    zv = jnp.zeros((d, MLA_KV_RANK, MLA_HEADS // 2, MLA_V), w_mla_ukv.dtype)
    wv = jnp.concatenate(
        [vv[:, :, :, 0], zv, zv, vv[:, :, :, 1]],
        axis=-1).reshape(d, MLA_KV_RANK, MLA_HEADS * HEAD_PAD).astype(BF16)
    return w_in_r, wa2, wq, wk, wv


def kernel(x, positions, norm_pre_mix, norm_post_mix, norm_pre_ffn, norm_post_ffn, w_in, w_pool,
           pool_scale, w_a, w_gla_a2, b_gla_a, gla_norm, w_b, mla_q_norm, w_mla_uq, mla_kv_norm,
           w_mla_ukv, w_c, w_o, w_ffn_gu, w_ffn_down):
    batch, seq, d_model = x.shape
    t = batch * seq
    x2 = x.reshape(t, d_model)
    c_tab, s_tab = _rope_tables(positions)
    w_in_r, wa2, wq, wk, wv = _prep_weights(w_in, w_gla_a2, w_mla_uq, w_mla_ukv)
    w_pool_b = w_pool.astype(BF16)
    w_a_b, w_b_b, w_c_b, w_o_b = (w.astype(BF16) for w in (w_a, w_b, w_c, w_o))
    w_g_b = w_ffn_gu[:, :, :D_FF].astype(BF16)
    w_u_b = w_ffn_gu[:, :, D_FF:].astype(BF16)
    w_d_b = w_ffn_down.astype(BF16)
    row = lambda a, l: a[l].reshape(1, -1)
    for l in range(DEPTH):
        u_pool, gq, gk, gv, gr, cq, ckv, misc, glog = _in_proj(x2, row(norm_pre_mix, l), w_in_r[l])
        pa = _pool(u_pool, w_pool_b[l], row(pool_scale, l), batch, seq)
        gb = _gla(gq, gk, gv, gr, misc, wa2[l], row(b_gla_a, l), row(gla_norm, l), batch, seq)
        mq, mk, mv = _mla_prep(cq, ckv, misc, c_tab, s_tab, row(mla_q_norm, l),
                               row(mla_kv_norm, l), wq[l], wk[l], wv[l])
        mc = _attention(mq, mk, mv, batch, seq)
        x2 = _merge(x2, pa, gb, mc, glog, w_a_b[l], w_b_b[l], w_c_b[l], w_o_b[l],
                    row(norm_post_mix, l))
        x2 = _ffn(x2, row(norm_pre_ffn, l), w_g_b[l], w_u_b[l], w_d_b[l], row(norm_post_ffn, l))
    return x2.reshape(batch, seq, d_model)
```

```python
import jax
import jax.numpy as jnp
from jax import lax
from jax.experimental import pallas as pl
from jax.experimental.pallas import tpu as pltpu

F32 = jnp.float32
BF16 = jnp.bfloat16

D_MODEL = 1024
DEPTH = 4
POOL_WIDTH = 512
POOL_WINDOWS = (2, 4, 8, 16)
POOL_GROUP_DIM = 128
GLA_HEADS = 4
GLA_DK = 64
GLA_DV = 128
GLA_KEY_WIDTH = GLA_HEADS * GLA_DK
GLA_VAL_WIDTH = GLA_HEADS * GLA_DV
GLA_GATE_RANK = 16
GLA_GATE_TAU = 16.0
GLA_CHUNK = 64
GLA_SUPER = 256
MLA_HEADS = 8
MLA_Q_RANK = 384
MLA_KV_RANK = 256
MLA_NOPE = 64
MLA_ROPE = 32
MLA_V = 64
MLA_QK = MLA_NOPE + MLA_ROPE
MLA_VAL_WIDTH = MLA_HEADS * MLA_V
ROPE_BASE = 10000.0
N_BRANCH = 3
D_FF = 2816
EPS = 1e-6

LANES = 128
HEAD_PAD = 128
VMEM_LIMIT = 56 * 1024 * 1024

IN_MAIN = 2048
IN_SEGS_MAIN = (("pool", POOL_WIDTH, F32), ("q", GLA_KEY_WIDTH, BF16), ("k", GLA_KEY_WIDTH, BF16),
                ("v", GLA_VAL_WIDTH, BF16), ("r", GLA_VAL_WIDTH, BF16))
IN_SEGS_MID = (("cq", MLA_Q_RANK, BF16), ("ckv", MLA_KV_RANK, BF16), ("misc", LANES, BF16))
IN_MID = sum(s[1] for s in IN_SEGS_MID)
IN_GATE = N_BRANCH * D_MODEL
IN_SEGS = IN_SEGS_MAIN + IN_SEGS_MID + (("glog", IN_GATE, BF16),)


def _dot(a, b):
    return jnp.dot(a, b, preferred_element_type=F32)


def _dot_nt(a, b):
    return lax.dot_general(a, b, (((1,), (1,)), ((), ())), preferred_element_type=F32)


def _dot_tn(a, b):
    return lax.dot_general(a, b, (((0,), (0,)), ((), ())), preferred_element_type=F32)


def _rms(x, g):
    return x * lax.rsqrt(jnp.mean(x * x, axis=-1, keepdims=True) + EPS) * g


def _sigmoid(x):
    return 0.5 * jnp.tanh(0.5 * x) + 0.5


def _params(n_axes):
    return pltpu.CompilerParams(
        dimension_semantics=("arbitrary",) * n_axes, vmem_limit_bytes=VMEM_LIMIT)


def _layer_spec(layer, tail, col=0):
    idx = (layer,) + (0,) * (len(tail) - 1) + (col,)
    return pl.BlockSpec((None,) + tuple(tail), lambda *_: idx, pipeline_mode=pl.Buffered(1))


def _rope_kernel(pos_ref, invf_ref, cos_ref, sin_ref):
    ang = pos_ref[...].astype(F32) * invf_ref[...]
    cos_ref[...] = jnp.cos(ang)
    sin_ref[...] = jnp.sin(ang)


def _rope_tables(positions):
    t = positions.size
    half = MLA_ROPE // 2
    inv_freq = ROPE_BASE ** (-jnp.arange(0, MLA_ROPE, 2, dtype=F32) / MLA_ROPE)
    rows = t * half // LANES
    pos_d = jnp.repeat(positions.reshape(t), half).reshape(rows, LANES)
    invf_d = jnp.tile(inv_freq, LANES // half).reshape(1, LANES)
    cos_d, sin_d = pl.pallas_call(
        _rope_kernel,
        out_shape=(jax.ShapeDtypeStruct((rows, LANES), F32),) * 2,
        name="rope_tables",
    )(pos_d, invf_d)
    cos = cos_d.reshape(t, half)
    sin = sin_d.reshape(t, half)
    one = jnp.ones((t, 32), F32)
    zero = jnp.zeros((t, half), F32)
    zero32 = jnp.zeros((t, 32), F32)
    c_tab = jnp.concatenate([one, cos, zero, one, cos, zero], axis=1)
    s_tab = jnp.concatenate([zero32, -sin, zero, zero32, sin, zero], axis=1)
    return c_tab, s_tab


def _in_proj_kernel(x_ref, g_ref, wm_ref, wd_ref, wg_ref, *out_refs):
    h = _rms(x_ref[...], g_ref[...]).astype(BF16)
    n_main, n_mid = len(IN_SEGS_MAIN), len(IN_SEGS_MID)
    groups = ((wm_ref, IN_SEGS_MAIN, out_refs[:n_main]),
              (wd_ref, IN_SEGS_MID, out_refs[n_main:n_main + n_mid]),
              (wg_ref, IN_SEGS[-1:], out_refs[n_main + n_mid:]))
    for w_ref, segs, refs in groups:
        c0 = 0
        for out_ref, (_, width, dtype) in zip(refs, segs):
            for a in range(0, width, 1024):
                b = min(a + 1024, width)
                out_ref[:, a:b] = _dot(h, w_ref[:, c0 + a:c0 + b]).astype(dtype)
            c0 += width


def _in_proj(x2, g, w_in_b, w_mid, w_gate, layer, tm=512):
    t = x2.shape[0]
    out_shape = tuple(jax.ShapeDtypeStruct((t, wd), dt) for _, wd, dt in IN_SEGS)
    out_specs = tuple(pl.BlockSpec((tm, wd), lambda i: (i, 0)) for _, wd, _ in IN_SEGS)
    return pl.pallas_call(
        _in_proj_kernel,
        grid=(t // tm,),
        in_specs=[
            pl.BlockSpec((tm, D_MODEL), lambda i: (i, 0)),
            _layer_spec(layer, (1, D_MODEL)),
            _layer_spec(layer, (D_MODEL, IN_MAIN)),
            _layer_spec(layer, (D_MODEL, IN_MID)),
            _layer_spec(layer, (D_MODEL, IN_GATE)),
        ],
        out_specs=out_specs,
        out_shape=out_shape,
        compiler_params=_params(1),
        name="in_proj",
    )(x2, g, w_in_b, w_mid, w_gate)


def _pool_kernel(u_ref, wp_ref, ps_ref, o_ref):
    s_len = u_ref.shape[0]
    t = lax.broadcasted_iota(jnp.int32, (s_len, POOL_GROUP_DIM), 0)
    for g, w in enumerate(POOL_WINDOWS):
        cols = slice(g * POOL_GROUP_DIM, (g + 1) * POOL_GROUP_DIM)
        u = u_ref[:, cols]
        s = u
        sh = 1
        while sh < w:
            s = s + jnp.where(t >= sh, pltpu.roll(s, sh, axis=0), 0.0)
            sh *= 2
        cnt = jnp.minimum(t + 1, w).astype(F32)
        diff = s / cnt - u
        y = _dot(diff.astype(BF16), wp_ref[g]) * ps_ref[:, cols]
        o_ref[:, cols] = y.astype(BF16)


def _pool(u, wp, ps, layer, batch, seq):
    t = u.shape[0]
    return pl.pallas_call(
        _pool_kernel,
        grid=(batch,),
        in_specs=[
            pl.BlockSpec((seq, POOL_WIDTH), lambda b: (b, 0)),
            _layer_spec(layer, wp.shape[1:]),
            _layer_spec(layer, (1, POOL_WIDTH)),
        ],
        out_specs=pl.BlockSpec((seq, POOL_WIDTH), lambda b: (b, 0)),
        out_shape=jax.ShapeDtypeStruct((t, POOL_WIDTH), BF16),
        compiler_params=_params(1),
        name="pool",
    )(u, wp, ps)


def _gla_kernel(q_ref, k_ref, v_ref, r_ref, misc_ref, wa2_ref, ba_ref, gn_ref, o_ref):
    s_len = q_ref.shape[0]
    c = GLA_CHUNK
    sc = GLA_SUPER
    row = lax.broadcasted_iota(jnp.int32, (sc, sc), 0)
    col = lax.broadcasted_iota(jnp.int32, (sc, sc), 1)
    same_chunk = (row // c) == (col // c)
    tril = same_chunk & (row >= col)
    tri_bf = tril.astype(BF16)
    ones_bf = same_chunk.astype(BF16)
    lane = lax.broadcasted_iota(jnp.int32, (1, 2 * GLA_DK), 1)
    head_lanes = [(lane >= hh * GLA_DK) & (lane < (hh + 1) * GLA_DK) for hh in range(2)]
    vcols = [slice(hh * GLA_DV, (hh + 1) * GLA_DV) for hh in range(2)]

    def body(n, st):
        rows = slice(n * sc, (n + 1) * sc)
        z = _dot(misc_ref[rows, :], wa2_ref[...]) + ba_ref[...]
        la = (jnp.minimum(z, 0.0) - jnp.log1p(jnp.exp(-jnp.abs(z)))) * (1.0 / GLA_GATE_TAU)
        hi = la.astype(BF16)
        r1 = la - hi.astype(F32)
        mid = r1.astype(BF16)
        lo = (r1 - mid.astype(F32)).astype(BF16)
        cum = _dot(tri_bf, hi) + _dot(tri_bf, mid) + _dot(tri_bf, lo)
        tot = _dot(ones_bf, hi) + _dot(ones_bf, mid) + _dot(ones_bf, lo)
        qf = q_ref[rows, :].astype(F32) * (GLA_DK ** -0.5)
        kf = k_ref[rows, :].astype(F32)
        q_dec = qf * jnp.exp(cum)
        k_inv = (kf * jnp.exp(-cum)).astype(BF16)
        k_end = kf * jnp.exp(tot - cum)
        dec = jnp.exp(tot)
        vh = [v_ref[rows, vcols[hh]] for hh in range(2)]
        ke = [jnp.where(head_lanes[hh], k_end, 0.0).astype(BF16) for hh in range(2)]
        sts = []
        for j in range(sc // c):
            sts.append(st.astype(BF16))
            rs = slice(j * c, (j + 1) * c)
            upd = _dot_tn(vh[0][rs], ke[0][rs]) + _dot_tn(vh[1][rs], ke[1][rs])
            st = st * dec[j * c:j * c + 1, :] + upd
        for hh in range(2):
            qd = jnp.where(head_lanes[hh], q_dec, 0.0).astype(BF16)
            att = jnp.where(tril, _dot_nt(qd, k_inv), 0.0).astype(BF16)
            o_inter = jnp.concatenate(
                [_dot_nt(qd[j * c:(j + 1) * c], sts[j]) for j in range(sc // c)], axis=0)
            o = _rms(_dot(att, vh[hh]) + o_inter, gn_ref[:, vcols[hh]])
            rr = r_ref[rows, vcols[hh]].astype(F32)
            o_ref[rows, vcols[hh]] = (o * (rr * _sigmoid(rr))).astype(BF16)
        return st

    st = jnp.zeros((GLA_DV, 2 * GLA_DK), F32)
    for n in range(s_len // sc):
        st = body(n, st)


def _gla(q, k, v, r, misc, wa2, ba, gn, layer, batch, seq):
    t = q.shape[0]
    kw = 2 * GLA_DK
    vw = 2 * GLA_DV
    return pl.pallas_call(
        _gla_kernel,
        grid=(batch, GLA_HEADS // 2),
        in_specs=[
            pl.BlockSpec((seq, kw), lambda b, p: (b, p)),
            pl.BlockSpec((seq, kw), lambda b, p: (b, p)),
            pl.BlockSpec((seq, vw), lambda b, p: (b, p)),
            pl.BlockSpec((seq, vw), lambda b, p: (b, p)),
            pl.BlockSpec((seq, LANES), lambda b, p: (b, 0)),
            pl.BlockSpec((None, LANES, kw), lambda b, p: (layer, 0, p)),
            pl.BlockSpec((None, 1, kw), lambda b, p: (layer, 0, p)),
            pl.BlockSpec((None, 1, vw), lambda b, p: (layer, 0, p)),
        ],
        out_specs=pl.BlockSpec((seq, vw), lambda b, p: (b, p)),
        out_shape=jax.ShapeDtypeStruct((t, GLA_VAL_WIDTH), BF16),
        compiler_params=_params(2),
        name="gla",
    )(q, k, v, r, misc, wa2, ba, gn)


def _mla_prep_kernel(cq_ref, ckv_ref, misc_ref, c_ref, s_ref, qn_ref, kvn_ref,
                     wq_ref, wk_ref, wv_ref, q_out, k_out, v_out):
    c_tab = c_ref[...]
    s_tab = s_ref[...]
    scale = MLA_QK ** -0.5
    cqn = _rms(cq_ref[...].astype(F32), qn_ref[...]).astype(BF16)
    qf = _dot(cqn, wq_ref[...])
    ckvn = _rms(ckv_ref[...].astype(F32), kvn_ref[...]).astype(BF16)
    kf = _dot(ckvn, wk_ref[...])
    v_out[...] = _dot(ckvn, wv_ref[...]).astype(BF16)
    lane = lax.broadcasted_iota(jnp.int32, (1, HEAD_PAD), 1)
    rope_lane = ((lane >= 32) & (lane < 48)) | ((lane >= 96) & (lane < 112))
    kr = misc_ref[...].astype(F32)
    kr = jnp.where(rope_lane, kr * c_tab + pltpu.roll(kr, 64, axis=1) * s_tab, 0.0)
    for h in range(MLA_HEADS):
        cols = slice(h * HEAD_PAD, (h + 1) * HEAD_PAD)
        x = qf[:, cols]
        q_out[:, cols] = ((x * c_tab + pltpu.roll(x, 64, axis=1) * s_tab) * scale).astype(BF16)
        k_out[:, cols] = (kf[:, cols] + kr).astype(BF16)


def _mla_prep(cq, ckv, misc, c_tab, s_tab, qn, kvn, wq, wk, wv, layer, tm=512):
    t = cq.shape[0]
    hw = MLA_HEADS * HEAD_PAD
    row = lambda w: pl.BlockSpec((tm, w), lambda i: (i, 0))
    return pl.pallas_call(
        _mla_prep_kernel,
        grid=(t // tm,),
        in_specs=[
            row(MLA_Q_RANK), row(MLA_KV_RANK), row(LANES), row(HEAD_PAD), row(HEAD_PAD),
            _layer_spec(layer, (1, MLA_Q_RANK)), _layer_spec(layer, (1, MLA_KV_RANK)),
            _layer_spec(layer, (MLA_Q_RANK, hw)), _layer_spec(layer, (MLA_KV_RANK, hw)),
            _layer_spec(layer, (MLA_KV_RANK, hw)),
        ],
        out_specs=(row(hw), row(hw), row(hw)),
        out_shape=(jax.ShapeDtypeStruct((t, hw), BF16),) * 3,
        compiler_params=_params(1),
        name="mla_prep",
    )(cq, ckv, misc, c_tab, s_tab, qn, kvn, wq, wk, wv)


ATT_TQ = 512
ATT_HALF = ATT_TQ // 2


def _attn_block(q_ref, k_ref, v_ref, qi, cols):
    tq, hf = ATT_TQ, ATT_HALF
    r0 = qi * tq
    neg = jnp.finfo(F32).min
    q = q_ref[r0:r0 + tq, cols]
    k_d = k_ref[r0:r0 + tq, cols]
    v_d = v_ref[r0:r0 + tq, cols]
    row0 = lax.broadcasted_iota(jnp.int32, (hf, hf), 0)
    col0 = lax.broadcasted_iota(jnp.int32, (hf, hf), 1)
    row1 = lax.broadcasted_iota(jnp.int32, (hf, tq), 0)
    col1 = lax.broadcasted_iota(jnp.int32, (hf, tq), 1)
    d0 = jnp.where(col0 <= row0, _dot_nt(q[:hf], k_d[:hf]), neg)
    d1 = jnp.where(col1 <= row1 + hf, _dot_nt(q[hf:], k_d), neg)
    m0 = jnp.max(d0, axis=-1, keepdims=True)
    m1 = jnp.max(d1, axis=-1, keepdims=True)
    if qi > 0:
        s_off = _dot_nt(q, k_ref[0:r0, cols])
        m0 = jnp.maximum(m0, jnp.max(s_off[:hf], axis=-1, keepdims=True))
        m1 = jnp.maximum(m1, jnp.max(s_off[hf:], axis=-1, keepdims=True))
        p_off0 = jnp.exp(s_off[:hf] - m0)
        p_off1 = jnp.exp(s_off[hf:] - m1)
        p_off = jnp.concatenate([p_off0.astype(BF16), p_off1.astype(BF16)], axis=0)
        o_off = _dot(p_off, v_ref[0:r0, cols])
    p0 = jnp.exp(d0 - m0)
    p1 = jnp.exp(d1 - m1)
    l0 = jnp.sum(p0, axis=-1, keepdims=True)
    l1 = jnp.sum(p1, axis=-1, keepdims=True)
    o0 = _dot(p0.astype(BF16), v_d[:hf])
    o1 = _dot(p1.astype(BF16), v_d)
    if qi > 0:
        l0 = l0 + jnp.sum(p_off0, axis=-1, keepdims=True)
        l1 = l1 + jnp.sum(p_off1, axis=-1, keepdims=True)
        o0 = o0 + o_off[:hf]
        o1 = o1 + o_off[hf:]
    return o0 / l0, o1 / l1


def _attn_kernel(q_ref, k_ref, v_ref, o_ref):
    tq, hf = ATT_TQ, ATT_HALF
    for qi in range(q_ref.shape[0] // tq):
        a0, a1 = _attn_block(q_ref, k_ref, v_ref, qi, slice(0, HEAD_PAD))
        b0, b1 = _attn_block(q_ref, k_ref, v_ref, qi, slice(HEAD_PAD, 2 * HEAD_PAD))
        o_ref[qi * tq:qi * tq + hf, :] = (a0 + b0).astype(BF16)
        o_ref[qi * tq + hf:(qi + 1) * tq, :] = (a1 + b1).astype(BF16)


def _attention(q, k, v, batch, seq):
    t = q.shape[0]
    pw = 2 * HEAD_PAD
    pair = pl.BlockSpec((seq, pw), lambda b, p: (b, p))
    return pl.pallas_call(
        _attn_kernel,
        grid=(batch, MLA_HEADS // 2),
        in_specs=[pair, pair, pair],
        out_specs=pl.BlockSpec((seq, HEAD_PAD), lambda b, p: (b, p)),
        out_shape=jax.ShapeDtypeStruct((t, MLA_VAL_WIDTH), BF16),
        compiler_params=_params(2),
        name="attention",
    )(q, k, v)


def _merge_kernel(x_ref, pa_ref, gb_ref, mc_ref, gl_ref, wa_ref, wb_ref, wc_ref, wo_ref,
                  n_ref, o_ref):
    merged = None
    for i, (b_ref, w_ref) in enumerate(((pa_ref, wa_ref), (gb_ref, wb_ref), (mc_ref, wc_ref))):
        y = _dot(b_ref[...], w_ref[...])
        gate = _sigmoid(gl_ref[:, i * D_MODEL:(i + 1) * D_MODEL].astype(F32))
        merged = gate * y if merged is None else merged + gate * y
    z = _dot(merged.astype(BF16), wo_ref[...])
    o_ref[...] = x_ref[...] + _rms(z, n_ref[...])


def _merge(x2, pa, gb, mc, glog, wa, wb, wc, wo, n_post, layer, tm=512):
    t = x2.shape[0]
    row = lambda w: pl.BlockSpec((tm, w), lambda i: (i, 0))
    return pl.pallas_call(
        _merge_kernel,
        grid=(t // tm,),
        in_specs=[
            row(D_MODEL), row(POOL_WIDTH), row(GLA_VAL_WIDTH), row(MLA_VAL_WIDTH),
            row(N_BRANCH * D_MODEL),
            _layer_spec(layer, (POOL_WIDTH, D_MODEL)), _layer_spec(layer, (GLA_VAL_WIDTH, D_MODEL)),
            _layer_spec(layer, (MLA_VAL_WIDTH, D_MODEL)), _layer_spec(layer, (D_MODEL, D_MODEL)),
            _layer_spec(layer, (1, D_MODEL)),
        ],
        out_specs=row(D_MODEL),
        out_shape=jax.ShapeDtypeStruct((t, D_MODEL), F32),
        compiler_params=_params(1),
        name="merge",
    )(x2, pa, gb, mc, glog, wa, wb, wc, wo, n_post)


FFN_CHUNKS = ((0, 1024), (1024, 2048), (2048, D_FF))


def _ffn_kernel(x_ref, n1_ref, wg_ref, wu_ref, wd_ref, n2_ref, o_ref):
    x = x_ref[...]
    h = _rms(x, n1_ref[...]).astype(BF16)
    acc = None
    for a, b in FFN_CHUNKS:
        g = _dot(h, wg_ref[:, a:b])
        u = _dot(h, wu_ref[:, a:b])
        act = (g * _sigmoid(g) * u).astype(BF16)
        d = _dot(act, wd_ref[a:b, :])
        acc = d if acc is None else acc + d
    o_ref[...] = x + _rms(acc, n2_ref[...])


def _ffn(x2, n1, wgu, wd, n2, layer, tm=512):
    t = x2.shape[0]
    row = pl.BlockSpec((tm, D_MODEL), lambda i: (i, 0))
    return pl.pallas_call(
        _ffn_kernel,
        grid=(t // tm,),
        in_specs=[
            row, _layer_spec(layer, (1, D_MODEL)),
            _layer_spec(layer, (D_MODEL, D_FF), col=0), _layer_spec(layer, (D_MODEL, D_FF), col=1),
            _layer_spec(layer, (D_FF, D_MODEL)), _layer_spec(layer, (1, D_MODEL)),
        ],
        out_specs=row,
        out_shape=jax.ShapeDtypeStruct((t, D_MODEL), F32),
        compiler_params=_params(1),
        name="ffn",
    )(x2, n1, wgu, wgu, wd, n2)


def _prep_weights(w_in, w_gla_a2, w_mla_uq, w_mla_ukv):
    sizes = (POOL_WIDTH, GLA_KEY_WIDTH, GLA_KEY_WIDTH, GLA_VAL_WIDTH, GLA_VAL_WIDTH,
             GLA_GATE_RANK, MLA_Q_RANK, MLA_KV_RANK, MLA_ROPE, N_BRANCH * D_MODEL)
    offs = [0]
    for s in sizes:
        offs.append(offs[-1] + s)
    assert offs[5] == IN_MAIN
    w_in_b = w_in.astype(BF16)
    w_a1, w_cq, w_ckv, w_kr, w_gate = (w_in_b[:, :, offs[i]:offs[i + 1]] for i in range(5, 10))
    d = w_in.shape[0]
    z16 = jnp.zeros((d, D_MODEL, 16), BF16)
    z48 = jnp.zeros((d, D_MODEL, 48), BF16)
    w_mid = jnp.concatenate(
        [w_cq, w_ckv, w_a1, z16, w_kr[..., :16], z48, w_kr[..., 16:], z16], axis=-1)

    wa2 = jnp.pad(w_gla_a2, ((0, 0), (0, LANES - GLA_GATE_RANK), (0, 0))).astype(BF16)

    uq = w_mla_uq.reshape(d, MLA_Q_RANK, MLA_HEADS, MLA_QK)
    zq = jnp.zeros((d, MLA_Q_RANK, MLA_HEADS, 16), w_mla_uq.dtype)
    wq = jnp.concatenate(
        [uq[..., 0:32], uq[..., 64:80], zq, uq[..., 32:64], uq[..., 80:96], zq],
        axis=-1).reshape(d, MLA_Q_RANK, MLA_HEADS * HEAD_PAD).astype(BF16)

    ukv = w_mla_ukv.reshape(d, MLA_KV_RANK, MLA_HEADS, MLA_NOPE + MLA_V)
    zk = jnp.zeros((d, MLA_KV_RANK, MLA_HEADS, 32), w_mla_ukv.dtype)
    wk = jnp.concatenate(
        [ukv[..., 0:32], zk, ukv[..., 32:64], zk],
        axis=-1).reshape(d, MLA_KV_RANK, MLA_HEADS * HEAD_PAD).astype(BF16)
    vv = ukv[..., MLA_NOPE:].reshape(d, MLA_KV_RANK, MLA_HEADS // 2, 2, MLA_V)
    zv = jnp.zeros((d, MLA_KV_RANK, MLA_HEADS // 2, MLA_V), w_mla_ukv.dtype)
    wv = jnp.concatenate(
        [vv[:, :, :, 0], zv, zv, vv[:, :, :, 1]],
        axis=-1).reshape(d, MLA_KV_RANK, MLA_HEADS * HEAD_PAD).astype(BF16)
    return w_in_b, w_mid, w_gate, wa2, wq, wk, wv


def kernel(x, positions, norm_pre_mix, norm_post_mix, norm_pre_ffn, norm_post_ffn, w_in, w_pool,
           pool_scale, w_a, w_gla_a2, b_gla_a, gla_norm, w_b, mla_q_norm, w_mla_uq, mla_kv_norm,
           w_mla_ukv, w_c, w_o, w_ffn_gu, w_ffn_down):
    batch, seq, d_model = x.shape
    t = batch * seq
    x2 = x.reshape(t, d_model)
    c_tab, s_tab = _rope_tables(positions)
    w_in_b, w_mid, w_gate, wa2, wq, wk, wv = _prep_weights(w_in, w_gla_a2, w_mla_uq, w_mla_ukv)
    w_pool_b = w_pool.astype(BF16)
    w_a_b, w_b_b, w_c_b, w_o_b = (w.astype(BF16) for w in (w_a, w_b, w_c, w_o))
    w_gu_b = w_ffn_gu.astype(BF16)
    w_d_b = w_ffn_down.astype(BF16)
    vec = lambda a: a.reshape(DEPTH, 1, -1)
    n_pre_mix, n_post_mix, n_pre_ffn, n_post_ffn = (
        vec(a) for a in (norm_pre_mix, norm_post_mix, norm_pre_ffn, norm_post_ffn))
    pool_scale, b_gla_a, gla_norm, mla_q_norm, mla_kv_norm = (
        vec(a) for a in (pool_scale, b_gla_a, gla_norm, mla_q_norm, mla_kv_norm))
    for l in range(DEPTH):
        u_pool, gq, gk, gv, gr, cq, ckv, misc, glog = _in_proj(
            x2, n_pre_mix, w_in_b, w_mid, w_gate, l)
        pa = _pool(u_pool, w_pool_b, pool_scale, l, batch, seq)
        gb = _gla(gq, gk, gv, gr, misc, wa2, b_gla_a, gla_norm, l, batch, seq)
        mq, mk, mv = _mla_prep(cq, ckv, misc, c_tab, s_tab, mla_q_norm, mla_kv_norm,
                               wq, wk, wv, l)
        mc = _attention(mq, mk, mv, batch, seq)
        x2 = _merge(x2, pa, gb, mc, glog, w_a_b, w_b_b, w_c_b, w_o_b, n_post_mix, l)
        x2 = _ffn(x2, n_pre_ffn, w_gu_b, w_d_b, n_post_ffn, l)
    return x2.reshape(batch, seq, d_model)
```

```python
import jax
import jax.numpy as jnp
import numpy as np
from jax import lax
from jax.experimental import pallas as pl
from jax.experimental.pallas import tpu as pltpu

F32 = jnp.float32
BF16 = jnp.bfloat16

D_MODEL = 1024
DEPTH = 4
POOL_WIDTH = 512
POOL_WINDOWS = (2, 4, 8, 16)
POOL_GROUP_DIM = 128
GLA_HEADS = 4
GLA_DK = 64
GLA_DV = 128
GLA_KEY_WIDTH = GLA_HEADS * GLA_DK
GLA_VAL_WIDTH = GLA_HEADS * GLA_DV
GLA_GATE_RANK = 16
GLA_GATE_TAU = 16.0
GLA_CHUNK = 64
GLA_SUPER = 256
MLA_HEADS = 8
MLA_Q_RANK = 384
MLA_KV_RANK = 256
MLA_NOPE = 64
MLA_ROPE = 32
MLA_V = 64
MLA_QK = MLA_NOPE + MLA_ROPE
MLA_VAL_WIDTH = MLA_HEADS * MLA_V
ROPE_BASE = 10000.0
N_BRANCH = 3
D_FF = 2816
EPS = 1e-6
LOG2E = 1.4426950408889634

LANES = 128
HEAD_PAD = 128
VMEM_LIMIT = 56 * 1024 * 1024

IN_MAIN = 2048
IN_SEGS_MAIN = (("pool", POOL_WIDTH, F32), ("q", GLA_KEY_WIDTH, BF16), ("k", GLA_KEY_WIDTH, BF16),
                ("v", GLA_VAL_WIDTH, BF16), ("r", GLA_VAL_WIDTH, BF16))
IN_SEGS_MID = (("cq", MLA_Q_RANK, BF16), ("ckv", MLA_KV_RANK, BF16), ("misc", LANES, BF16))
IN_MID = sum(s[1] for s in IN_SEGS_MID)
IN_GATE = N_BRANCH * D_MODEL
IN_SEGS = IN_SEGS_MAIN + IN_SEGS_MID + (("glog", IN_GATE, BF16),)


def _dot(a, b):
    return jnp.dot(a, b, preferred_element_type=F32)


def _dot_nt(a, b):
    return lax.dot_general(a, b, (((1,), (1,)), ((), ())), preferred_element_type=F32)


def _dot_tn(a, b):
    return lax.dot_general(a, b, (((0,), (0,)), ((), ())), preferred_element_type=F32)


def _rms(x, g):
    return x * lax.rsqrt(jnp.mean(x * x, axis=-1, keepdims=True) + EPS) * g


def _sigmoid(x):
    return 0.5 * jnp.tanh(0.5 * x) + 0.5


def _params(n_axes):
    return pltpu.CompilerParams(
        dimension_semantics=("arbitrary",) * n_axes, vmem_limit_bytes=VMEM_LIMIT)


def _layer_spec(layer, tail, col=0):
    idx = (layer,) + (0,) * (len(tail) - 1) + (col,)
    return pl.BlockSpec((None,) + tuple(tail), lambda *_: idx, pipeline_mode=pl.Buffered(1))


ROPE_HALF = MLA_ROPE // 2
TOK_PER_ROW = LANES // ROPE_HALF


def _split3(x):
    hi = x.astype(BF16)
    r1 = x - hi.astype(F32)
    mid = r1.astype(BF16)
    lo = (r1 - mid.astype(F32)).astype(BF16)
    return hi, mid, lo


def _rope_kernel(pos_ref, invf_ref, ec_ref, es_ref, base_ref, c_ref, s_ref):
    ang = pos_ref[...].astype(F32) * invf_ref[...]
    cos3 = _split3(jnp.cos(ang))
    sin3 = _split3(jnp.sin(ang))
    for j in range(TOK_PER_ROW):
        ec, es = ec_ref[j], es_ref[j]
        c_ref[:, j, :] = sum(_dot(t, ec) for t in cos3) + base_ref[...]
        s_ref[:, j, :] = sum(_dot(t, es) for t in sin3)


def _rope_tables(positions):
    t = positions.size
    inv_freq = ROPE_BASE ** (-jnp.arange(0, MLA_ROPE, 2, dtype=F32) / MLA_ROPE)
    rows = t // TOK_PER_ROW
    pos_d = jnp.repeat(positions.reshape(t), ROPE_HALF).reshape(rows, LANES)
    invf_d = jnp.tile(inv_freq, TOK_PER_ROW).reshape(1, LANES)
    ec = np.zeros((TOK_PER_ROW, LANES, HEAD_PAD), np.float32)
    es = np.zeros((TOK_PER_ROW, LANES, HEAD_PAD), np.float32)
    for j in range(TOK_PER_ROW):
        for i in range(ROPE_HALF):
            ec[j, j * ROPE_HALF + i, 32 + i] = 1.0
            ec[j, j * ROPE_HALF + i, 96 + i] = 1.0
            es[j, j * ROPE_HALF + i, 32 + i] = -1.0
            es[j, j * ROPE_HALF + i, 96 + i] = 1.0
    base = np.zeros((1, HEAD_PAD), np.float32)
    base[0, 0:32] = 1.0
    base[0, 64:96] = 1.0
    c3, s3 = pl.pallas_call(
        _rope_kernel,
        out_shape=(jax.ShapeDtypeStruct((rows, TOK_PER_ROW, HEAD_PAD), F32),) * 2,
        compiler_params=pltpu.CompilerParams(vmem_limit_bytes=VMEM_LIMIT),
        name="rope_tables",
    )(pos_d, invf_d, jnp.asarray(ec, BF16), jnp.asarray(es, BF16), jnp.asarray(base))
    return c3.reshape(t, HEAD_PAD), s3.reshape(t, HEAD_PAD)


def _in_proj_kernel(x_ref, g_ref, wm_ref, wd_ref, wg_ref, *out_refs):
    h = _rms(x_ref[...], g_ref[...]).astype(BF16)
    n_main, n_mid = len(IN_SEGS_MAIN), len(IN_SEGS_MID)
    groups = ((wm_ref, IN_SEGS_MAIN, out_refs[:n_main]),
              (wd_ref, IN_SEGS_MID, out_refs[n_main:n_main + n_mid]),
              (wg_ref, IN_SEGS[-1:], out_refs[n_main + n_mid:]))
    for w_ref, segs, refs in groups:
        c0 = 0
        for out_ref, (_, width, dtype) in zip(refs, segs):
            for a in range(0, width, 1024):
                b = min(a + 1024, width)
                out_ref[:, a:b] = _dot(h, w_ref[:, c0 + a:c0 + b]).astype(dtype)
            c0 += width


def _in_proj(x2, g, w_in_b, w_mid, w_gate, layer, tm=512):
    t = x2.shape[0]
    out_shape = tuple(jax.ShapeDtypeStruct((t, wd), dt) for _, wd, dt in IN_SEGS)
    out_specs = tuple(pl.BlockSpec((tm, wd), lambda i: (i, 0)) for _, wd, _ in IN_SEGS)
    return pl.pallas_call(
        _in_proj_kernel,
        grid=(t // tm,),
        in_specs=[
            pl.BlockSpec((tm, D_MODEL), lambda i: (i, 0)),
            _layer_spec(layer, (1, D_MODEL)),
            _layer_spec(layer, (D_MODEL, IN_MAIN)),
            _layer_spec(layer, (D_MODEL, IN_MID)),
            _layer_spec(layer, (D_MODEL, IN_GATE)),
        ],
        out_specs=out_specs,
        out_shape=out_shape,
        compiler_params=_params(1),
        name="in_proj",
    )(x2, g, w_in_b, w_mid, w_gate)


def _pool_kernel(u_ref, wp_ref, ps_ref, o_ref):
    s_len = u_ref.shape[0]
    t = lax.broadcasted_iota(jnp.int32, (s_len, POOL_GROUP_DIM), 0)
    for g, w in enumerate(POOL_WINDOWS):
        cols = slice(g * POOL_GROUP_DIM, (g + 1) * POOL_GROUP_DIM)
        u = u_ref[:, cols]
        s = u
        sh = 1
        while sh < w:
            s = s + jnp.where(t >= sh, pltpu.roll(s, sh, axis=0), 0.0)
            sh *= 2
        cnt = jnp.minimum(t + 1, w).astype(F32)
        diff = s / cnt - u
        y = _dot(diff.astype(BF16), wp_ref[g]) * ps_ref[:, cols]
        o_ref[:, cols] = y.astype(BF16)


def _pool(u, wp, ps, layer, batch, seq):
    t = u.shape[0]
    return pl.pallas_call(
        _pool_kernel,
        grid=(batch,),
        in_specs=[
            pl.BlockSpec((seq, POOL_WIDTH), lambda b: (b, 0)),
            _layer_spec(layer, wp.shape[1:]),
            _layer_spec(layer, (1, POOL_WIDTH)),
        ],
        out_specs=pl.BlockSpec((seq, POOL_WIDTH), lambda b: (b, 0)),
        out_shape=jax.ShapeDtypeStruct((t, POOL_WIDTH), BF16),
        compiler_params=_params(1),
        name="pool",
    )(u, wp, ps)


def _gla_kernel(q_ref, k_ref, v_ref, r_ref, misc_ref, wa2_ref, ba_ref, gn_ref, o_ref):
    s_len = q_ref.shape[0]
    c = GLA_CHUNK
    sc = GLA_SUPER
    row = lax.broadcasted_iota(jnp.int32, (sc, sc), 0)
    col = lax.broadcasted_iota(jnp.int32, (sc, sc), 1)
    same_chunk = (row // c) == (col // c)
    tril = same_chunk & (row >= col)
    tri_bf = tril.astype(BF16)
    lane = lax.broadcasted_iota(jnp.int32, (1, 2 * GLA_DK), 1)
    head_lanes = [(lane >= hh * GLA_DK) & (lane < (hh + 1) * GLA_DK) for hh in range(2)]
    vcols = [slice(hh * GLA_DV, (hh + 1) * GLA_DV) for hh in range(2)]
    blk = lax.broadcasted_iota(jnp.int32, (sc, 2 * GLA_DK), 0) // c
    nsub = sc // c

    def spread(x):
        return jnp.concatenate(
            [jnp.where(blk == j, x, 0.0).astype(BF16) for j in range(nsub)], axis=1)

    blocks = [slice(n * sc, (n + 1) * sc) for n in range(s_len // sc)]

    z = _dot(misc_ref[...], wa2_ref[...]) + ba_ref[...]
    la = (jnp.minimum(z, 0.0) - jnp.log1p(jnp.exp(-jnp.abs(z)))) * (1.0 / GLA_GATE_TAU)
    la3 = jnp.concatenate(_split3(la), axis=1)
    cum3 = [_dot(tri_bf, la3[rows]) for rows in blocks]

    q_dec, k_inv, k_end, last = [], [], [], []
    for rows, c3 in zip(blocks, cum3):
        cum = c3[:, :LANES] + c3[:, LANES:2 * LANES] + c3[:, 2 * LANES:]
        ends = [cum[(j + 1) * c - 1:(j + 1) * c, :] for j in range(nsub)]
        tot = jnp.concatenate([jnp.broadcast_to(r, (c, 2 * GLA_DK)) for r in ends], axis=0)
        qf = q_ref[rows, :].astype(F32) * (GLA_DK ** -0.5)
        kf = k_ref[rows, :].astype(F32)
        q_dec.append(qf * jnp.exp(cum))
        k_inv.append((kf * jnp.exp(-cum)).astype(BF16))
        k_end.append(kf * jnp.exp(tot - cum))
        last.append(ends)

    upd = [sum(_dot_tn(v_ref[rows, vcols[hh]], spread(jnp.where(head_lanes[hh], ke, 0.0)))
               for hh in range(2)) for rows, ke in zip(blocks, k_end)]
    att = [[jnp.where(tril, _dot_nt(jnp.where(head_lanes[hh], qd, 0.0).astype(BF16), ki),
                      0.0).astype(BF16) for hh in range(2)] for qd, ki in zip(q_dec, k_inv)]

    st = jnp.zeros((GLA_DV, 2 * GLA_DK), F32)
    st_cat = []
    for ends, u in zip(last, upd):
        sts = []
        for j in range(nsub):
            sts.append(st.astype(BF16))
            st = st * jnp.exp(ends[j]) + u[:, j * LANES:(j + 1) * LANES]
        st_cat.append(jnp.concatenate(sts, axis=1))

    for rows, qd, a, sc_n in zip(blocks, q_dec, att, st_cat):
        for hh in range(2):
            o = _dot(a[hh], v_ref[rows, vcols[hh]])
            o = o + _dot_nt(spread(jnp.where(head_lanes[hh], qd, 0.0)), sc_n)
            o = _rms(o, gn_ref[:, vcols[hh]])
            rr = r_ref[rows, vcols[hh]].astype(F32)
            o_ref[rows, vcols[hh]] = (o * (rr * _sigmoid(rr))).astype(BF16)


def _gla(q, k, v, r, misc, wa2, ba, gn, layer, batch, seq):
    t = q.shape[0]
    kw = 2 * GLA_DK
    vw = 2 * GLA_DV
    return pl.pallas_call(
        _gla_kernel,
        grid=(batch, GLA_HEADS // 2),
        in_specs=[
            pl.BlockSpec((seq, kw), lambda b, p: (b, p)),
            pl.BlockSpec((seq, kw), lambda b, p: (b, p)),
            pl.BlockSpec((seq, vw), lambda b, p: (b, p)),
            pl.BlockSpec((seq, vw), lambda b, p: (b, p)),
            pl.BlockSpec((seq, LANES), lambda b, p: (b, 0)),
            pl.BlockSpec((None, LANES, kw), lambda b, p: (layer, 0, p)),
            pl.BlockSpec((None, 1, kw), lambda b, p: (layer, 0, p)),
            pl.BlockSpec((None, 1, vw), lambda b, p: (layer, 0, p)),
        ],
        out_specs=pl.BlockSpec((seq, vw), lambda b, p: (b, p)),
        out_shape=jax.ShapeDtypeStruct((t, GLA_VAL_WIDTH), BF16),
        compiler_params=_params(2),
        name="gla",
    )(q, k, v, r, misc, wa2, ba, gn)


def _mla_prep_kernel(cq_ref, ckv_ref, misc_ref, c_ref, s_ref, qn_ref, kvn_ref,
                     wq_ref, wk_ref, wv_ref, q_out, k_out, v_out):
    c_tab = c_ref[...]
    s_tab = s_ref[...]
    scale = MLA_QK ** -0.5 * LOG2E
    cqn = _rms(cq_ref[...].astype(F32), qn_ref[...]).astype(BF16)
    qf = _dot(cqn, wq_ref[...])
    ckvn = _rms(ckv_ref[...].astype(F32), kvn_ref[...]).astype(BF16)
    kf = _dot(ckvn, wk_ref[...])
    v_out[...] = _dot(ckvn, wv_ref[...]).astype(BF16)
    lane = lax.broadcasted_iota(jnp.int32, (1, HEAD_PAD), 1)
    rope_lane = ((lane >= 32) & (lane < 48)) | ((lane >= 96) & (lane < 112))
    kr = misc_ref[...].astype(F32)
    kr = jnp.where(rope_lane, kr * c_tab + pltpu.roll(kr, 64, axis=1) * s_tab, 0.0)
    for h in range(MLA_HEADS):
        cols = slice(h * HEAD_PAD, (h + 1) * HEAD_PAD)
        x = qf[:, cols]
        q_out[:, cols] = ((x * c_tab + pltpu.roll(x, 64, axis=1) * s_tab) * scale).astype(BF16)
        k_out[:, cols] = (kf[:, cols] + kr).astype(BF16)


def _mla_prep(cq, ckv, misc, c_tab, s_tab, qn, kvn, wq, wk, wv, layer, tm=512):
    t = cq.shape[0]
    hw = MLA_HEADS * HEAD_PAD
    row = lambda w: pl.BlockSpec((tm, w), lambda i: (i, 0))
    return pl.pallas_call(
        _mla_prep_kernel,
        grid=(t // tm,),
        in_specs=[
            row(MLA_Q_RANK), row(MLA_KV_RANK), row(LANES), row(HEAD_PAD), row(HEAD_PAD),
            _layer_spec(layer, (1, MLA_Q_RANK)), _layer_spec(layer, (1, MLA_KV_RANK)),
            _layer_spec(layer, (MLA_Q_RANK, hw)), _layer_spec(layer, (MLA_KV_RANK, hw)),
            _layer_spec(layer, (MLA_KV_RANK, hw)),
        ],
        out_specs=(row(hw), row(hw), row(hw)),
        out_shape=(jax.ShapeDtypeStruct((t, hw), BF16),) * 3,
        compiler_params=_params(1),
        name="mla_prep",
    )(cq, ckv, misc, c_tab, s_tab, qn, kvn, wq, wk, wv)


ATT_TQ = 512
ATT_HALF = ATT_TQ // 2


def _attn_scores(q_ref, k_ref, qi, cols):
    tq, hf = ATT_TQ, ATT_HALF
    r0 = qi * tq
    q = q_ref[r0:r0 + tq, cols]
    k_d = k_ref[r0:r0 + tq, cols]
    d0 = _dot_nt(q[:hf], k_d[:hf])
    d1 = _dot_nt(q[hf:], k_d)
    s_off = _dot_nt(q, k_ref[0:r0, cols]) if qi > 0 else None
    return d0, d1, s_off


def _attn_finish(v_ref, qi, cols, scores):
    tq, hf = ATT_TQ, ATT_HALF
    r0 = qi * tq
    neg = jnp.finfo(F32).min
    d0, d1, s_off = scores
    v_d = v_ref[r0:r0 + tq, cols]
    row0 = lax.broadcasted_iota(jnp.int32, (hf, hf), 0)
    col0 = lax.broadcasted_iota(jnp.int32, (hf, hf), 1)
    row1 = lax.broadcasted_iota(jnp.int32, (hf, tq), 0)
    col1 = lax.broadcasted_iota(jnp.int32, (hf, tq), 1)
    d0 = jnp.where(col0 <= row0, d0, neg)
    d1 = jnp.where(col1 <= row1 + hf, d1, neg)
    m0 = jnp.max(d0, axis=-1, keepdims=True)
    m1 = jnp.max(d1, axis=-1, keepdims=True)
    if s_off is not None:
        m0 = jnp.maximum(m0, jnp.max(s_off[:hf], axis=-1, keepdims=True))
        m1 = jnp.maximum(m1, jnp.max(s_off[hf:], axis=-1, keepdims=True))
        p_off0 = jnp.exp2(s_off[:hf] - m0)
        p_off1 = jnp.exp2(s_off[hf:] - m1)
        p_off = jnp.concatenate([p_off0.astype(BF16), p_off1.astype(BF16)], axis=0)
        o_off = _dot(p_off, v_ref[0:r0, cols])
    p0 = jnp.exp2(d0 - m0)
    p1 = jnp.exp2(d1 - m1)
    l0 = jnp.sum(p0, axis=-1, keepdims=True)
    l1 = jnp.sum(p1, axis=-1, keepdims=True)
    o0 = _dot(p0.astype(BF16), v_d[:hf])
    o1 = _dot(p1.astype(BF16), v_d)
    if s_off is not None:
        l0 = l0 + jnp.sum(p_off0, axis=-1, keepdims=True)
        l1 = l1 + jnp.sum(p_off1, axis=-1, keepdims=True)
        o0 = o0 + o_off[:hf]
        o1 = o1 + o_off[hf:]
    return o0 / l0, o1 / l1


def _attn_kernel(q_ref, k_ref, v_ref, o_ref):
    tq, hf = ATT_TQ, ATT_HALF
    head_cols = (slice(0, HEAD_PAD), slice(HEAD_PAD, 2 * HEAD_PAD))
    units = [(qi, hh) for qi in reversed(range(q_ref.shape[0] // tq)) for hh in range(2)]
    scores = _attn_scores(q_ref, k_ref, units[0][0], head_cols[units[0][1]])
    prev = None
    for i, (qi, hh) in enumerate(units):
        nxt = None
        if i + 1 < len(units):
            nxt = _attn_scores(q_ref, k_ref, units[i + 1][0], head_cols[units[i + 1][1]])
        cur = _attn_finish(v_ref, qi, head_cols[hh], scores)
        scores = nxt
        if hh == 0:
            prev = cur
        else:
            o_ref[qi * tq:qi * tq + hf, :] = (prev[0] + cur[0]).astype(BF16)
            o_ref[qi * tq + hf:(qi + 1) * tq, :] = (prev[1] + cur[1]).astype(BF16)


def _attention(q, k, v, batch, seq):
    t = q.shape[0]
    pw = 2 * HEAD_PAD
    pair = pl.BlockSpec((seq, pw), lambda b, p: (b, p))
    return pl.pallas_call(
        _attn_kernel,
        grid=(batch, MLA_HEADS // 2),
        in_specs=[pair, pair, pair],
        out_specs=pl.BlockSpec((seq, HEAD_PAD), lambda b, p: (b, p)),
        out_shape=jax.ShapeDtypeStruct((t, MLA_VAL_WIDTH), BF16),
        compiler_params=_params(2),
        name="attention",
    )(q, k, v)


def _merge_kernel(x_ref, pa_ref, gb_ref, mc_ref, gl_ref, wa_ref, wb_ref, wc_ref, wo_ref,
                  n_ref, o_ref):
    merged = None
    for i, (b_ref, w_ref) in enumerate(((pa_ref, wa_ref), (gb_ref, wb_ref), (mc_ref, wc_ref))):
        y = _dot(b_ref[...], w_ref[...])
        gate = _sigmoid(gl_ref[:, i * D_MODEL:(i + 1) * D_MODEL].astype(F32))
        merged = gate * y if merged is None else merged + gate * y
    z = _dot(merged.astype(BF16), wo_ref[...])
    o_ref[...] = x_ref[...] + _rms(z, n_ref[...])


def _merge(x2, pa, gb, mc, glog, wa, wb, wc, wo, n_post, layer, tm=512):
    t = x2.shape[0]
    row = lambda w: pl.BlockSpec((tm, w), lambda i: (i, 0))
    return pl.pallas_call(
        _merge_kernel,
        grid=(t // tm,),
        in_specs=[
            row(D_MODEL), row(POOL_WIDTH), row(GLA_VAL_WIDTH), row(MLA_VAL_WIDTH),
            row(N_BRANCH * D_MODEL),
            _layer_spec(layer, (POOL_WIDTH, D_MODEL)), _layer_spec(layer, (GLA_VAL_WIDTH, D_MODEL)),
            _layer_spec(layer, (MLA_VAL_WIDTH, D_MODEL)), _layer_spec(layer, (D_MODEL, D_MODEL)),
            _layer_spec(layer, (1, D_MODEL)),
        ],
        out_specs=row(D_MODEL),
        out_shape=jax.ShapeDtypeStruct((t, D_MODEL), F32),
        compiler_params=_params(1),
        name="merge",
    )(x2, pa, gb, mc, glog, wa, wb, wc, wo, n_post)


FFN_CHUNKS = ((0, 1024), (1024, 2048), (2048, D_FF))


def _ffn_kernel(x_ref, n1_ref, wg_ref, wu_ref, wd_ref, n2_ref, o_ref):
    x = x_ref[...]
    h = _rms(x, n1_ref[...]).astype(BF16)
    acc = None
    for a, b in FFN_CHUNKS:
        g = _dot(h, wg_ref[:, a:b])
        u = _dot(h, wu_ref[:, a:b])
        act = (g * _sigmoid(g) * u).astype(BF16)
        d = _dot(act, wd_ref[a:b, :])
        acc = d if acc is None else acc + d
    o_ref[...] = x + _rms(acc, n2_ref[...])


def _ffn(x2, n1, wgu, wd, n2, layer, tm=512):
    t = x2.shape[0]
    row = pl.BlockSpec((tm, D_MODEL), lambda i: (i, 0))
    return pl.pallas_call(
        _ffn_kernel,
        grid=(t // tm,),
        in_specs=[
            row, _layer_spec(layer, (1, D_MODEL)),
            _layer_spec(layer, (D_MODEL, D_FF), col=0), _layer_spec(layer, (D_MODEL, D_FF), col=1),
            _layer_spec(layer, (D_FF, D_MODEL)), _layer_spec(layer, (1, D_MODEL)),
        ],
        out_specs=row,
        out_shape=jax.ShapeDtypeStruct((t, D_MODEL), F32),
        compiler_params=_params(1),
        name="ffn",
    )(x2, n1, wgu, wgu, wd, n2)


def _prep_weights(w_in, w_gla_a2, w_mla_uq, w_mla_ukv):
    sizes = (POOL_WIDTH, GLA_KEY_WIDTH, GLA_KEY_WIDTH, GLA_VAL_WIDTH, GLA_VAL_WIDTH,
             GLA_GATE_RANK, MLA_Q_RANK, MLA_KV_RANK, MLA_ROPE, N_BRANCH * D_MODEL)
    offs = [0]
    for s in sizes:
        offs.append(offs[-1] + s)
    assert offs[5] == IN_MAIN
    w_in_b = w_in.astype(BF16)
    w_a1, w_cq, w_ckv, w_kr, w_gate = (w_in_b[:, :, offs[i]:offs[i + 1]] for i in range(5, 10))
    d = w_in.shape[0]
    z16 = jnp.zeros((d, D_MODEL, 16), BF16)
    z48 = jnp.zeros((d, D_MODEL, 48), BF16)
    w_mid = jnp.concatenate(
        [w_cq, w_ckv, w_a1, z16, w_kr[..., :16], z48, w_kr[..., 16:], z16], axis=-1)

    wa2 = jnp.pad(w_gla_a2, ((0, 0), (0, LANES - GLA_GATE_RANK), (0, 0))).astype(BF16)

    uq = w_mla_uq.reshape(d, MLA_Q_RANK, MLA_HEADS, MLA_QK)
    zq = jnp.zeros((d, MLA_Q_RANK, MLA_HEADS, 16), w_mla_uq.dtype)
    wq = jnp.concatenate(
        [uq[..., 0:32], uq[..., 64:80], zq, uq[..., 32:64], uq[..., 80:96], zq],
        axis=-1).reshape(d, MLA_Q_RANK, MLA_HEADS * HEAD_PAD).astype(BF16)

    ukv = w_mla_ukv.reshape(d, MLA_KV_RANK, MLA_HEADS, MLA_NOPE + MLA_V)
    zk = jnp.zeros((d, MLA_KV_RANK, MLA_HEADS, 32), w_mla_ukv.dtype)
    wk = jnp.concatenate(
        [ukv[..., 0:32], zk, ukv[..., 32:64], zk],
        axis=-1).reshape(d, MLA_KV_RANK, MLA_HEADS * HEAD_PAD).astype(BF16)
    vv = ukv[..., MLA_NOPE:].reshape(d, MLA_KV_RANK, MLA_HEADS // 2, 2, MLA_V)
    zv = jnp.zeros((d, MLA_KV_RANK, MLA_HEADS // 2, MLA_V), w_mla_ukv.dtype)
    wv = jnp.concatenate(
        [vv[:, :, :, 0], zv, zv, vv[:, :, :, 1]],
        axis=-1).reshape(d, MLA_KV_RANK, MLA_HEADS * HEAD_PAD).astype(BF16)
    return w_in_b, w_mid, w_gate, wa2, wq, wk, wv


def kernel(x, positions, norm_pre_mix, norm_post_mix, norm_pre_ffn, norm_post_ffn, w_in, w_pool,
           pool_scale, w_a, w_gla_a2, b_gla_a, gla_norm, w_b, mla_q_norm, w_mla_uq, mla_kv_norm,
           w_mla_ukv, w_c, w_o, w_ffn_gu, w_ffn_down):
    batch, seq, d_model = x.shape
    t = batch * seq
    x2 = x.reshape(t, d_model)
    c_tab, s_tab = _rope_tables(positions)
    w_in_b, w_mid, w_gate, wa2, wq, wk, wv = _prep_weights(w_in, w_gla_a2, w_mla_uq, w_mla_ukv)
    w_pool_b = w_pool.astype(BF16)
    w_a_b, w_b_b, w_c_b, w_o_b = (w.astype(BF16) for w in (w_a, w_b, w_c, w_o))
    w_gu_b = w_ffn_gu.astype(BF16)
    w_d_b = w_ffn_down.astype(BF16)
    vec = lambda a: a.reshape(DEPTH, 1, -1)
    n_pre_mix, n_post_mix, n_pre_ffn, n_post_ffn = (
        vec(a) for a in (norm_pre_mix, norm_post_mix, norm_pre_ffn, norm_post_ffn))
    pool_scale, b_gla_a, gla_norm, mla_q_norm, mla_kv_norm = (
        vec(a) for a in (pool_scale, b_gla_a, gla_norm, mla_q_norm, mla_kv_norm))
    for l in range(DEPTH):
        u_pool, gq, gk, gv, gr, cq, ckv, misc, glog = _in_proj(
            x2, n_pre_mix, w_in_b, w_mid, w_gate, l)
        pa = _pool(u_pool, w_pool_b, pool_scale, l, batch, seq)
        gb = _gla(gq, gk, gv, gr, misc, wa2, b_gla_a, gla_norm, l, batch, seq)
        mq, mk, mv = _mla_prep(cq, ckv, misc, c_tab, s_tab, mla_q_norm, mla_kv_norm,
                               wq, wk, wv, l)
        mc = _attention(mq, mk, mv, batch, seq)
        x2 = _merge(x2, pa, gb, mc, glog, w_a_b, w_b_b, w_c_b, w_o_b, n_post_mix, l)
        x2 = _ffn(x2, n_pre_ffn, w_gu_b, w_d_b, n_post_ffn, l)
    return x2.reshape(batch, seq, d_model)
```

```python
import jax
import jax.numpy as jnp
import numpy as np
from jax import lax
from jax.experimental import pallas as pl
from jax.experimental.pallas import tpu as pltpu

F32 = jnp.float32
BF16 = jnp.bfloat16

D_MODEL = 1024
DEPTH = 4
POOL_WIDTH = 512
POOL_WINDOWS = (2, 4, 8, 16)
POOL_GROUP_DIM = 128
GLA_HEADS = 4
GLA_DK = 64
GLA_DV = 128
GLA_KEY_WIDTH = GLA_HEADS * GLA_DK
GLA_VAL_WIDTH = GLA_HEADS * GLA_DV
GLA_GATE_RANK = 16
GLA_GATE_TAU = 16.0
GLA_CHUNK = 64
GLA_SUPER = 256
MLA_HEADS = 8
MLA_Q_RANK = 384
MLA_KV_RANK = 256
MLA_NOPE = 64
MLA_ROPE = 32
MLA_V = 64
MLA_QK = MLA_NOPE + MLA_ROPE
MLA_VAL_WIDTH = MLA_HEADS * MLA_V
ROPE_BASE = 10000.0
N_BRANCH = 3
D_FF = 2816
EPS = 1e-6
LOG2E = 1.4426950408889634

LANES = 128
HEAD_PAD = 128
VMEM_LIMIT = 56 * 1024 * 1024

IN_MAIN = 2048
IN_SEGS_MAIN = (("pool", POOL_WIDTH, F32), ("q", GLA_KEY_WIDTH, BF16), ("k", GLA_KEY_WIDTH, BF16),
                ("v", GLA_VAL_WIDTH, BF16), ("r", GLA_VAL_WIDTH, BF16))
IN_SEGS_MID = (("cq", MLA_Q_RANK, BF16), ("ckv", MLA_KV_RANK, BF16), ("misc", LANES, BF16))
IN_MID = sum(s[1] for s in IN_SEGS_MID)
IN_GATE = N_BRANCH * D_MODEL
IN_SEGS = IN_SEGS_MAIN + IN_SEGS_MID + (("glog", IN_GATE, BF16),)


def _dot(a, b):
    return jnp.dot(a, b, preferred_element_type=F32)


def _dot_nt(a, b):
    return lax.dot_general(a, b, (((1,), (1,)), ((), ())), preferred_element_type=F32)


def _dot_tn(a, b):
    return lax.dot_general(a, b, (((0,), (0,)), ((), ())), preferred_element_type=F32)


def _rms(x, g):
    return x * lax.rsqrt(jnp.mean(x * x, axis=-1, keepdims=True) + EPS) * g


def _sigmoid(x):
    return 0.5 * jnp.tanh(0.5 * x) + 0.5


def _params(n_axes):
    return pltpu.CompilerParams(
        dimension_semantics=("arbitrary",) * n_axes, vmem_limit_bytes=VMEM_LIMIT)


def _layer_spec(layer, tail, col=0):
    idx = (layer,) + (0,) * (len(tail) - 1) + (col,)
    return pl.BlockSpec((None,) + tuple(tail), lambda *_: idx, pipeline_mode=pl.Buffered(1))


ROPE_HALF = MLA_ROPE // 2
TOK_PER_ROW = LANES // ROPE_HALF


def _split3(x):
    hi = x.astype(BF16)
    r1 = x - hi.astype(F32)
    mid = r1.astype(BF16)
    lo = (r1 - mid.astype(F32)).astype(BF16)
    return hi, mid, lo


def _rope_kernel(pos_ref, invf_ref, ec_ref, es_ref, base_ref, c_ref, s_ref):
    ang = pos_ref[...].astype(F32) * invf_ref[...]
    cos3 = _split3(jnp.cos(ang))
    sin3 = _split3(jnp.sin(ang))
    for j in range(TOK_PER_ROW):
        ec, es = ec_ref[j], es_ref[j]
        c_ref[:, j, :] = sum(_dot(t, ec) for t in cos3) + base_ref[...]
        s_ref[:, j, :] = sum(_dot(t, es) for t in sin3)


def _rope_tables(positions):
    t = positions.size
    inv_freq = ROPE_BASE ** (-jnp.arange(0, MLA_ROPE, 2, dtype=F32) / MLA_ROPE)
    rows = t // TOK_PER_ROW
    pos_d = jnp.repeat(positions.reshape(t), ROPE_HALF).reshape(rows, LANES)
    invf_d = jnp.tile(inv_freq, TOK_PER_ROW).reshape(1, LANES)
    ec = np.zeros((TOK_PER_ROW, LANES, HEAD_PAD), np.float32)
    es = np.zeros((TOK_PER_ROW, LANES, HEAD_PAD), np.float32)
    for j in range(TOK_PER_ROW):
        for i in range(ROPE_HALF):
            ec[j, j * ROPE_HALF + i, 32 + i] = 1.0
            ec[j, j * ROPE_HALF + i, 96 + i] = 1.0
            es[j, j * ROPE_HALF + i, 32 + i] = -1.0
            es[j, j * ROPE_HALF + i, 96 + i] = 1.0
    base = np.zeros((1, HEAD_PAD), np.float32)
    base[0, 0:32] = 1.0
    base[0, 64:96] = 1.0
    c3, s3 = pl.pallas_call(
        _rope_kernel,
        out_shape=(jax.ShapeDtypeStruct((rows, TOK_PER_ROW, HEAD_PAD), F32),) * 2,
        compiler_params=pltpu.CompilerParams(vmem_limit_bytes=VMEM_LIMIT),
        name="rope_tables",
    )(pos_d, invf_d, jnp.asarray(ec, BF16), jnp.asarray(es, BF16), jnp.asarray(base))
    return c3.reshape(t, HEAD_PAD), s3.reshape(t, HEAD_PAD)


def _in_proj_kernel(x_ref, g_ref, wm_ref, wd_ref, wg_ref, *out_refs):
    h = _rms(x_ref[...], g_ref[...]).astype(BF16)
    n_main, n_mid = len(IN_SEGS_MAIN), len(IN_SEGS_MID)
    groups = ((wm_ref, IN_SEGS_MAIN, out_refs[:n_main]),
              (wd_ref, IN_SEGS_MID, out_refs[n_main:n_main + n_mid]),
              (wg_ref, IN_SEGS[-1:], out_refs[n_main + n_mid:]))
    for w_ref, segs, refs in groups:
        total = sum(s[1] for s in segs)
        for a in range(0, total, 1024):
            b = min(a + 1024, total)
            y = _dot(h, w_ref[:, a:b])
            c0 = 0
            for out_ref, (_, width, dtype) in zip(refs, segs):
                lo, hi = max(a, c0), min(b, c0 + width)
                if lo < hi:
                    out_ref[:, lo - c0:hi - c0] = y[:, lo - a:hi - a].astype(dtype)
                c0 += width


def _in_proj(x2, g, w_main, w_mid, w_gate, layer, tm=512):
    t = x2.shape[0]
    out_shape = tuple(jax.ShapeDtypeStruct((t, wd), dt) for _, wd, dt in IN_SEGS)
    out_specs = tuple(pl.BlockSpec((tm, wd), lambda i: (i, 0)) for _, wd, _ in IN_SEGS)
    return pl.pallas_call(
        _in_proj_kernel,
        grid=(t // tm,),
        in_specs=[
            pl.BlockSpec((tm, D_MODEL), lambda i: (i, 0)),
            _layer_spec(layer, (1, D_MODEL)),
            _layer_spec(layer, (D_MODEL, IN_MAIN)),
            _layer_spec(layer, (D_MODEL, IN_MID)),
            _layer_spec(layer, (D_MODEL, IN_GATE)),
        ],
        out_specs=out_specs,
        out_shape=out_shape,
        compiler_params=_params(1),
        name="in_proj",
    )(x2, g, w_main, w_mid, w_gate)


def _pool_kernel(u_ref, wp_ref, ps_ref, o_ref):
    s_len = u_ref.shape[0]
    t = lax.broadcasted_iota(jnp.int32, (s_len, POOL_GROUP_DIM), 0)
    for g, w in enumerate(POOL_WINDOWS):
        cols = slice(g * POOL_GROUP_DIM, (g + 1) * POOL_GROUP_DIM)
        u = u_ref[:, cols]
        s = u
        sh = 1
        while sh < w:
            s = s + jnp.where(t >= sh, pltpu.roll(s, sh, axis=0), 0.0)
            sh *= 2
        cnt = jnp.minimum(t + 1, w).astype(F32)
        diff = s / cnt - u
        y = _dot(diff.astype(BF16), wp_ref[g]) * ps_ref[:, cols]
        o_ref[:, cols] = y.astype(BF16)


def _pool(u, wp, ps, layer, batch, seq):
    t = u.shape[0]
    return pl.pallas_call(
        _pool_kernel,
        grid=(batch,),
        in_specs=[
            pl.BlockSpec((seq, POOL_WIDTH), lambda b: (b, 0)),
            _layer_spec(layer, wp.shape[1:]),
            _layer_spec(layer, (1, POOL_WIDTH)),
        ],
        out_specs=pl.BlockSpec((seq, POOL_WIDTH), lambda b: (b, 0)),
        out_shape=jax.ShapeDtypeStruct((t, POOL_WIDTH), BF16),
        compiler_params=_params(1),
        name="pool",
    )(u, wp, ps)


def _gla_kernel(q_ref, k_ref, v_ref, r_ref, misc_ref, wa2_ref, ba_ref, gn_ref, o_ref):
    s_len = q_ref.shape[0]
    c = GLA_CHUNK
    sc = GLA_SUPER
    row = lax.broadcasted_iota(jnp.int32, (sc, sc), 0)
    col = lax.broadcasted_iota(jnp.int32, (sc, sc), 1)
    same_chunk = (row // c) == (col // c)
    tril = same_chunk & (row >= col)
    tri_bf = tril.astype(BF16)
    lane = lax.broadcasted_iota(jnp.int32, (1, 2 * GLA_DK), 1)
    head_lanes = [(lane >= hh * GLA_DK) & (lane < (hh + 1) * GLA_DK) for hh in range(2)]
    vcols = [slice(hh * GLA_DV, (hh + 1) * GLA_DV) for hh in range(2)]
    blk = lax.broadcasted_iota(jnp.int32, (sc, 2 * GLA_DK), 0) // c
    nsub = sc // c

    def spread(x):
        return jnp.concatenate(
            [jnp.where(blk == j, x, 0.0).astype(BF16) for j in range(nsub)], axis=1)

    blocks = [slice(n * sc, (n + 1) * sc) for n in range(s_len // sc)]

    z = _dot(misc_ref[...], wa2_ref[...]) + ba_ref[...]
    la = (jnp.minimum(z, 0.0) - jnp.log1p(jnp.exp(-jnp.abs(z)))) * (1.0 / GLA_GATE_TAU)
    la3 = jnp.concatenate(_split3(la), axis=1)
    cum3 = [_dot(tri_bf, la3[rows]) for rows in blocks]

    q_dec, k_inv, k_end, last = [], [], [], []
    for rows, c3 in zip(blocks, cum3):
        cum = c3[:, :LANES] + c3[:, LANES:2 * LANES] + c3[:, 2 * LANES:]
        ends = [cum[(j + 1) * c - 1:(j + 1) * c, :] for j in range(nsub)]
        tot = jnp.concatenate([jnp.broadcast_to(r, (c, 2 * GLA_DK)) for r in ends], axis=0)
        qf = q_ref[rows, :].astype(F32) * (GLA_DK ** -0.5)
        kf = k_ref[rows, :].astype(F32)
        q_dec.append(qf * jnp.exp(cum))
        k_inv.append((kf * jnp.exp(-cum)).astype(BF16))
        k_end.append(kf * jnp.exp(tot - cum))
        last.append(ends)

    upd = [sum(_dot_tn(v_ref[rows, vcols[hh]], spread(jnp.where(head_lanes[hh], ke, 0.0)))
               for hh in range(2)) for rows, ke in zip(blocks, k_end)]
    att = [[jnp.where(tril, _dot_nt(jnp.where(head_lanes[hh], qd, 0.0).astype(BF16), ki),
                      0.0).astype(BF16) for hh in range(2)] for qd, ki in zip(q_dec, k_inv)]

    st = jnp.zeros((GLA_DV, 2 * GLA_DK), F32)
    st_cat = []
    for ends, u in zip(last, upd):
        sts = []
        for j in range(nsub):
            sts.append(st.astype(BF16))
            st = st * jnp.exp(ends[j]) + u[:, j * LANES:(j + 1) * LANES]
        st_cat.append(jnp.concatenate(sts, axis=1))

    for rows, qd, a, sc_n in zip(blocks, q_dec, att, st_cat):
        for hh in range(2):
            o = _dot(a[hh], v_ref[rows, vcols[hh]])
            o = o + _dot_nt(spread(jnp.where(head_lanes[hh], qd, 0.0)), sc_n)
            o = _rms(o, gn_ref[:, vcols[hh]])
            rr = r_ref[rows, vcols[hh]].astype(F32)
            o_ref[rows, vcols[hh]] = (o * (rr * _sigmoid(rr))).astype(BF16)


def _gla(q, k, v, r, misc, wa2, ba, gn, layer, batch, seq):
    t = q.shape[0]
    kw = 2 * GLA_DK
    vw = 2 * GLA_DV
    return pl.pallas_call(
        _gla_kernel,
        grid=(batch, GLA_HEADS // 2),
        in_specs=[
            pl.BlockSpec((seq, kw), lambda b, p: (b, p)),
            pl.BlockSpec((seq, kw), lambda b, p: (b, p)),
            pl.BlockSpec((seq, vw), lambda b, p: (b, p)),
            pl.BlockSpec((seq, vw), lambda b, p: (b, p)),
            pl.BlockSpec((seq, LANES), lambda b, p: (b, 0)),
            pl.BlockSpec((None, LANES, kw), lambda b, p: (layer, 0, p)),
            pl.BlockSpec((None, 1, kw), lambda b, p: (layer, 0, p)),
            pl.BlockSpec((None, 1, vw), lambda b, p: (layer, 0, p)),
        ],
        out_specs=pl.BlockSpec((seq, vw), lambda b, p: (b, p)),
        out_shape=jax.ShapeDtypeStruct((t, GLA_VAL_WIDTH), BF16),
        compiler_params=_params(2),
        name="gla",
    )(q, k, v, r, misc, wa2, ba, gn)


def _mla_prep_kernel(cq_ref, ckv_ref, misc_ref, c_ref, s_ref, qn_ref, kvn_ref,
                     wq_ref, wk_ref, wv_ref, q_out, k_out, v_out):
    c_tab = c_ref[...]
    s_tab = s_ref[...]
    scale = MLA_QK ** -0.5 * LOG2E
    cqn = _rms(cq_ref[...].astype(F32), qn_ref[...]).astype(BF16)
    qf = _dot(cqn, wq_ref[...])
    ckvn = _rms(ckv_ref[...].astype(F32), kvn_ref[...]).astype(BF16)
    kf = _dot(ckvn, wk_ref[...])
    v_out[...] = _dot(ckvn, wv_ref[...]).astype(BF16)
    lane = lax.broadcasted_iota(jnp.int32, (1, HEAD_PAD), 1)
    rope_lane = ((lane >= 32) & (lane < 48)) | ((lane >= 96) & (lane < 112))
    kr = misc_ref[...].astype(F32)
    kr = jnp.where(rope_lane, kr * c_tab + pltpu.roll(kr, 64, axis=1) * s_tab, 0.0)
    for h in range(MLA_HEADS):
        cols = slice(h * HEAD_PAD, (h + 1) * HEAD_PAD)
        x = qf[:, cols]
        q_out[:, cols] = ((x * c_tab + pltpu.roll(x, 64, axis=1) * s_tab) * scale).astype(BF16)
        k_out[:, cols] = (kf[:, cols] + kr).astype(BF16)


def _mla_prep(cq, ckv, misc, c_tab, s_tab, qn, kvn, wq, wk, wv, layer, tm=512):
    t = cq.shape[0]
    hw = MLA_HEADS * HEAD_PAD
    row = lambda w: pl.BlockSpec((tm, w), lambda i: (i, 0))
    return pl.pallas_call(
        _mla_prep_kernel,
        grid=(t // tm,),
        in_specs=[
            row(MLA_Q_RANK), row(MLA_KV_RANK), row(LANES), row(HEAD_PAD), row(HEAD_PAD),
            _layer_spec(layer, (1, MLA_Q_RANK)), _layer_spec(layer, (1, MLA_KV_RANK)),
            _layer_spec(layer, (MLA_Q_RANK, hw)), _layer_spec(layer, (MLA_KV_RANK, hw)),
            _layer_spec(layer, (MLA_KV_RANK, MLA_VAL_WIDTH)),
        ],
        out_specs=(row(hw), row(hw), row(MLA_VAL_WIDTH)),
        out_shape=(jax.ShapeDtypeStruct((t, hw), BF16), jax.ShapeDtypeStruct((t, hw), BF16),
                   jax.ShapeDtypeStruct((t, MLA_VAL_WIDTH), BF16)),
        compiler_params=_params(1),
        name="mla_prep",
    )(cq, ckv, misc, c_tab, s_tab, qn, kvn, wq, wk, wv)


ATT_TQ = 512
ATT_HALF = ATT_TQ // 2


def _attn_scores(q_ref, k_ref, qi, cols):
    tq, hf = ATT_TQ, ATT_HALF
    r0 = qi * tq
    q = q_ref[r0:r0 + tq, cols]
    k_d = k_ref[r0:r0 + tq, cols]
    d0 = _dot_nt(q[:hf], k_d[:hf])
    d1 = _dot_nt(q[hf:], k_d)
    s_off = _dot_nt(q, k_ref[0:r0, cols]) if qi > 0 else None
    return d0, d1, s_off


def _attn_finish(v, qi, scores):
    tq, hf = ATT_TQ, ATT_HALF
    r0 = qi * tq
    neg = jnp.finfo(F32).min
    d0, d1, s_off = scores
    v_d = v[r0:r0 + tq]
    row0 = lax.broadcasted_iota(jnp.int32, (hf, hf), 0)
    col0 = lax.broadcasted_iota(jnp.int32, (hf, hf), 1)
    row1 = lax.broadcasted_iota(jnp.int32, (hf, tq), 0)
    col1 = lax.broadcasted_iota(jnp.int32, (hf, tq), 1)
    d0 = jnp.where(col0 <= row0, d0, neg)
    d1 = jnp.where(col1 <= row1 + hf, d1, neg)
    m0 = jnp.max(d0, axis=-1, keepdims=True)
    m1 = jnp.max(d1, axis=-1, keepdims=True)
    if s_off is not None:
        m0 = jnp.maximum(m0, jnp.max(s_off[:hf], axis=-1, keepdims=True))
        m1 = jnp.maximum(m1, jnp.max(s_off[hf:], axis=-1, keepdims=True))
        p_off0 = jnp.exp2(s_off[:hf] - m0)
        p_off1 = jnp.exp2(s_off[hf:] - m1)
        p_off = jnp.concatenate([p_off0.astype(BF16), p_off1.astype(BF16)], axis=0)
        o_off = _dot(p_off, v[0:r0])
    p0 = jnp.exp2(d0 - m0)
    p1 = jnp.exp2(d1 - m1)
    l0 = jnp.sum(p0, axis=-1, keepdims=True)
    l1 = jnp.sum(p1, axis=-1, keepdims=True)
    o0 = _dot(p0.astype(BF16), v_d[:hf])
    o1 = _dot(p1.astype(BF16), v_d)
    if s_off is not None:
        l0 = l0 + jnp.sum(p_off0, axis=-1, keepdims=True)
        l1 = l1 + jnp.sum(p_off1, axis=-1, keepdims=True)
        o0 = o0 + o_off[:hf]
        o1 = o1 + o_off[hf:]
    return o0 / l0, o1 / l1


def _attn_kernel(q_ref, k_ref, v_ref, o_ref):
    tq, hf = ATT_TQ, ATT_HALF
    head_cols = (slice(0, HEAD_PAD), slice(HEAD_PAD, 2 * HEAD_PAD))
    lane = lax.broadcasted_iota(jnp.int32, (1, HEAD_PAD), 1)
    v_pair = v_ref[...].astype(F32)
    v_head = (jnp.where(lane < MLA_V, v_pair, 0.0).astype(BF16),
              jnp.where(lane >= MLA_V, v_pair, 0.0).astype(BF16))
    units = [(qi, hh) for qi in reversed(range(q_ref.shape[0] // tq)) for hh in range(2)]
    scores = _attn_scores(q_ref, k_ref, units[0][0], head_cols[units[0][1]])
    prev = None
    for i, (qi, hh) in enumerate(units):
        nxt = None
        if i + 1 < len(units):
            nxt = _attn_scores(q_ref, k_ref, units[i + 1][0], head_cols[units[i + 1][1]])
        cur = _attn_finish(v_head[hh], qi, scores)
        scores = nxt
        if hh == 0:
            prev = cur
        else:
            o_ref[qi * tq:qi * tq + hf, :] = (prev[0] + cur[0]).astype(BF16)
            o_ref[qi * tq + hf:(qi + 1) * tq, :] = (prev[1] + cur[1]).astype(BF16)


def _attention(q, k, v, batch, seq):
    t = q.shape[0]
    pw = 2 * HEAD_PAD
    pair = pl.BlockSpec((seq, pw), lambda b, p: (b, p))
    return pl.pallas_call(
        _attn_kernel,
        grid=(batch, MLA_HEADS // 2),
        in_specs=[pair, pair, pl.BlockSpec((seq, HEAD_PAD), lambda b, p: (b, p))],
        out_specs=pl.BlockSpec((seq, HEAD_PAD), lambda b, p: (b, p)),
        out_shape=jax.ShapeDtypeStruct((t, MLA_VAL_WIDTH), BF16),
        compiler_params=_params(2),
        name="attention",
    )(q, k, v)


FFN_CHUNKS = ((0, 1024), (1024, 2048), (2048, D_FF))
MERGE_FFN_VMEM = 60 * 1024 * 1024


def _merge_ffn_kernel(x_ref, pa_ref, gb_ref, mc_ref, gl_ref, wa_ref, wb_ref, wc_ref, wo_ref,
                      n_mix_ref, n1_ref, wg_ref, wu_ref, wd_ref, n2_ref, o_ref):
    merged = None
    for i, (b_ref, w_ref) in enumerate(((pa_ref, wa_ref), (gb_ref, wb_ref), (mc_ref, wc_ref))):
        y = _dot(b_ref[...], w_ref[...])
        gate = _sigmoid(gl_ref[:, i * D_MODEL:(i + 1) * D_MODEL].astype(F32))
        merged = gate * y if merged is None else merged + gate * y
    z = _dot(merged.astype(BF16), wo_ref[...])
    x = x_ref[...] + _rms(z, n_mix_ref[...])
    h = _rms(x, n1_ref[...]).astype(BF16)
    acc = None
    for a, b in FFN_CHUNKS:
        g = _dot(h, wg_ref[:, a:b])
        u = _dot(h, wu_ref[:, a:b])
        act = (g * _sigmoid(g) * u).astype(BF16)
        d = _dot(act, wd_ref[a:b, :])
        acc = d if acc is None else acc + d
    o_ref[...] = x + _rms(acc, n2_ref[...])


def _merge_ffn(x2, pa, gb, mc, glog, wa, wb, wc, wo, n_mix, n1, wgu, wd, n2, layer, tm=512):
    t = x2.shape[0]
    row = lambda w: pl.BlockSpec((tm, w), lambda i: (i, 0))
    return pl.pallas_call(
        _merge_ffn_kernel,
        grid=(t // tm,),
        in_specs=[
            row(D_MODEL), row(POOL_WIDTH), row(GLA_VAL_WIDTH), row(MLA_VAL_WIDTH),
            row(N_BRANCH * D_MODEL),
            _layer_spec(layer, (POOL_WIDTH, D_MODEL)), _layer_spec(layer, (GLA_VAL_WIDTH, D_MODEL)),
            _layer_spec(layer, (MLA_VAL_WIDTH, D_MODEL)), _layer_spec(layer, (D_MODEL, D_MODEL)),
            _layer_spec(layer, (1, D_MODEL)), _layer_spec(layer, (1, D_MODEL)),
            _layer_spec(layer, (D_MODEL, D_FF), col=0), _layer_spec(layer, (D_MODEL, D_FF), col=1),
            _layer_spec(layer, (D_FF, D_MODEL)), _layer_spec(layer, (1, D_MODEL)),
        ],
        out_specs=row(D_MODEL),
        out_shape=jax.ShapeDtypeStruct((t, D_MODEL), F32),
        compiler_params=pltpu.CompilerParams(
            dimension_semantics=("arbitrary",), vmem_limit_bytes=MERGE_FFN_VMEM),
        name="merge_ffn",
    )(x2, pa, gb, mc, glog, wa, wb, wc, wo, n_mix, n1, wgu, wgu, wd, n2)


def _prep_weights(w_in, w_gla_a2, w_mla_uq, w_mla_ukv):
    sizes = (POOL_WIDTH, GLA_KEY_WIDTH, GLA_KEY_WIDTH, GLA_VAL_WIDTH, GLA_VAL_WIDTH,
             GLA_GATE_RANK, MLA_Q_RANK, MLA_KV_RANK, MLA_ROPE, N_BRANCH * D_MODEL)
    offs = [0]
    for s in sizes:
        offs.append(offs[-1] + s)
    assert offs[5] == IN_MAIN
    w_main = w_in[:, :, :IN_MAIN].astype(BF16)
    w_a1, w_cq, w_ckv, w_kr, w_gate = (w_in[:, :, offs[i]:offs[i + 1]] for i in range(5, 10))
    w_gate = w_gate.astype(BF16)
    d = w_in.shape[0]
    z16 = jnp.zeros((d, D_MODEL, 16), w_in.dtype)
    z48 = jnp.zeros((d, D_MODEL, 48), w_in.dtype)
    w_mid = jnp.concatenate(
        [w_cq, w_ckv, w_a1, z16, w_kr[..., :16], z48, w_kr[..., 16:], z16], axis=-1).astype(BF16)

    wa2 = jnp.pad(w_gla_a2, ((0, 0), (0, LANES - GLA_GATE_RANK), (0, 0))).astype(BF16)

    uq = w_mla_uq.reshape(d, MLA_Q_RANK, MLA_HEADS, MLA_QK)
    zq = jnp.zeros((d, MLA_Q_RANK, MLA_HEADS, 16), w_mla_uq.dtype)
    wq = jnp.concatenate(
        [uq[..., 0:32], uq[..., 64:80], zq, uq[..., 32:64], uq[..., 80:96], zq],
        axis=-1).reshape(d, MLA_Q_RANK, MLA_HEADS * HEAD_PAD).astype(BF16)

    ukv = w_mla_ukv.reshape(d, MLA_KV_RANK, MLA_HEADS, MLA_NOPE + MLA_V)
    zk = jnp.zeros((d, MLA_KV_RANK, MLA_HEADS, 32), w_mla_ukv.dtype)
    wk = jnp.concatenate(
        [ukv[..., 0:32], zk, ukv[..., 32:64], zk],
        axis=-1).reshape(d, MLA_KV_RANK, MLA_HEADS * HEAD_PAD).astype(BF16)
    wv = ukv[..., MLA_NOPE:].reshape(d, MLA_KV_RANK, MLA_VAL_WIDTH).astype(BF16)
    return w_main, w_mid, w_gate, wa2, wq, wk, wv


def kernel(x, positions, norm_pre_mix, norm_post_mix, norm_pre_ffn, norm_post_ffn, w_in, w_pool,
           pool_scale, w_a, w_gla_a2, b_gla_a, gla_norm, w_b, mla_q_norm, w_mla_uq, mla_kv_norm,
           w_mla_ukv, w_c, w_o, w_ffn_gu, w_ffn_down):
    batch, seq, d_model = x.shape
    t = batch * seq
    x2 = x.reshape(t, d_model)
    c_tab, s_tab = _rope_tables(positions)
    w_main, w_mid, w_gate, wa2, wq, wk, wv = _prep_weights(w_in, w_gla_a2, w_mla_uq, w_mla_ukv)
    w_pool_b = w_pool.astype(BF16)
    w_a_b, w_b_b, w_c_b, w_o_b = (w.astype(BF16) for w in (w_a, w_b, w_c, w_o))
    w_gu_b = w_ffn_gu.astype(BF16)
    w_d_b = w_ffn_down.astype(BF16)
    vec = lambda a: a.reshape(DEPTH, 1, -1)
    n_pre_mix, n_post_mix, n_pre_ffn, n_post_ffn = (
        vec(a) for a in (norm_pre_mix, norm_post_mix, norm_pre_ffn, norm_post_ffn))
    pool_scale, b_gla_a, gla_norm, mla_q_norm, mla_kv_norm = (
        vec(a) for a in (pool_scale, b_gla_a, gla_norm, mla_q_norm, mla_kv_norm))
    for l in range(DEPTH):
        u_pool, gq, gk, gv, gr, cq, ckv, misc, glog = _in_proj(
            x2, n_pre_mix, w_main, w_mid, w_gate, l)
        pa = _pool(u_pool, w_pool_b, pool_scale, l, batch, seq)
        gb = _gla(gq, gk, gv, gr, misc, wa2, b_gla_a, gla_norm, l, batch, seq)
        mq, mk, mv = _mla_prep(cq, ckv, misc, c_tab, s_tab, mla_q_norm, mla_kv_norm,
                               wq, wk, wv, l)
        mc = _attention(mq, mk, mv, batch, seq)
        x2 = _merge_ffn(x2, pa, gb, mc, glog, w_a_b, w_b_b, w_c_b, w_o_b, n_post_mix,
                        n_pre_ffn, w_gu_b, w_d_b, n_post_ffn, l)
    return x2.reshape(batch, seq, d_model)
```

```python
import jax
import jax.numpy as jnp
import numpy as np
from jax import lax
from jax.experimental import pallas as pl
from jax.experimental.pallas import tpu as pltpu

F32 = jnp.float32
BF16 = jnp.bfloat16

D_MODEL = 1024
DEPTH = 4
POOL_WIDTH = 512
POOL_WINDOWS = (2, 4, 8, 16)
POOL_GROUP_DIM = 128
GLA_HEADS = 4
GLA_DK = 64
GLA_DV = 128
GLA_KEY_WIDTH = GLA_HEADS * GLA_DK
GLA_VAL_WIDTH = GLA_HEADS * GLA_DV
GLA_GATE_RANK = 16
GLA_GATE_TAU = 16.0
GLA_CHUNK = 64
GLA_SUPER = 256
MLA_HEADS = 8
MLA_Q_RANK = 384
MLA_KV_RANK = 256
MLA_NOPE = 64
MLA_ROPE = 32
MLA_V = 64
MLA_QK = MLA_NOPE + MLA_ROPE
MLA_VAL_WIDTH = MLA_HEADS * MLA_V
ROPE_BASE = 10000.0
N_BRANCH = 3
D_FF = 2816
EPS = 1e-6
LOG2E = 1.4426950408889634

LANES = 128
HEAD_PAD = 128
VMEM_LIMIT = 56 * 1024 * 1024

IN_MAIN = 2048
IN_SEGS_MAIN = (("pool", POOL_WIDTH, F32), ("q", GLA_KEY_WIDTH, BF16), ("k", GLA_KEY_WIDTH, BF16),
                ("v", GLA_VAL_WIDTH, BF16), ("r", GLA_VAL_WIDTH, BF16))
IN_SEGS_MID = (("cq", MLA_Q_RANK, BF16), ("ckv", MLA_KV_RANK, BF16), ("misc", LANES, BF16))
IN_MID = sum(s[1] for s in IN_SEGS_MID)
IN_GATE = N_BRANCH * D_MODEL
IN_SEGS = IN_SEGS_MAIN + IN_SEGS_MID + (("glog", IN_GATE, BF16),)


def _dot(a, b):
    return jnp.dot(a, b, preferred_element_type=F32)


def _dot_nt(a, b):
    return lax.dot_general(a, b, (((1,), (1,)), ((), ())), preferred_element_type=F32)


def _dot_tn(a, b):
    return lax.dot_general(a, b, (((0,), (0,)), ((), ())), preferred_element_type=F32)


def _rms(x, g):
    return x * lax.rsqrt(jnp.mean(x * x, axis=-1, keepdims=True) + EPS) * g


def _sigmoid(x):
    return 0.5 * jnp.tanh(0.5 * x) + 0.5


def _params(n_axes):
    return pltpu.CompilerParams(
        dimension_semantics=("arbitrary",) * n_axes, vmem_limit_bytes=VMEM_LIMIT)


def _layer_spec(layer, tail, col=0):
    idx = (layer,) + (0,) * (len(tail) - 1) + (col,)
    return pl.BlockSpec((None,) + tuple(tail), lambda *_: idx, pipeline_mode=pl.Buffered(1))


ROPE_HALF = MLA_ROPE // 2
TOK_PER_ROW = LANES // ROPE_HALF


def _split3(x):
    hi = x.astype(BF16)
    r1 = x - hi.astype(F32)
    mid = r1.astype(BF16)
    lo = (r1 - mid.astype(F32)).astype(BF16)
    return hi, mid, lo


def _rope_kernel(pos_ref, invf_ref, ec_ref, es_ref, base_ref, c_ref, s_ref):
    ang = pos_ref[...].astype(F32) * invf_ref[...]
    cos3 = _split3(jnp.cos(ang))
    sin3 = _split3(jnp.sin(ang))
    for j in range(TOK_PER_ROW):
        ec, es = ec_ref[j], es_ref[j]
        c_ref[:, j, :] = sum(_dot(t, ec) for t in cos3) + base_ref[...]
        s_ref[:, j, :] = sum(_dot(t, es) for t in sin3)


def _rope_tables(positions):
    t = positions.size
    inv_freq = ROPE_BASE ** (-jnp.arange(0, MLA_ROPE, 2, dtype=F32) / MLA_ROPE)
    rows = t // TOK_PER_ROW
    pos_d = jnp.repeat(positions.reshape(t), ROPE_HALF).reshape(rows, LANES)
    invf_d = jnp.tile(inv_freq, TOK_PER_ROW).reshape(1, LANES)
    ec = np.zeros((TOK_PER_ROW, LANES, HEAD_PAD), np.float32)
    es = np.zeros((TOK_PER_ROW, LANES, HEAD_PAD), np.float32)
    for j in range(TOK_PER_ROW):
        for i in range(ROPE_HALF):
            ec[j, j * ROPE_HALF + i, 32 + i] = 1.0
            ec[j, j * ROPE_HALF + i, 96 + i] = 1.0
            es[j, j * ROPE_HALF + i, 32 + i] = -1.0
            es[j, j * ROPE_HALF + i, 96 + i] = 1.0
    base = np.zeros((1, HEAD_PAD), np.float32)
    base[0, 0:32] = 1.0
    base[0, 64:96] = 1.0
    c3, s3 = pl.pallas_call(
        _rope_kernel,
        out_shape=(jax.ShapeDtypeStruct((rows, TOK_PER_ROW, HEAD_PAD), F32),) * 2,
        compiler_params=pltpu.CompilerParams(vmem_limit_bytes=VMEM_LIMIT),
        name="rope_tables",
    )(pos_d, invf_d, jnp.asarray(ec, BF16), jnp.asarray(es, BF16), jnp.asarray(base))
    return c3.reshape(t, HEAD_PAD), s3.reshape(t, HEAD_PAD)


def _in_proj_kernel(x_ref, g_ref, wm_ref, wd_ref, wg_ref, *out_refs):
    h = _rms(x_ref[...], g_ref[...]).astype(BF16)
    n_main, n_mid = len(IN_SEGS_MAIN), len(IN_SEGS_MID)
    groups = ((wm_ref, IN_SEGS_MAIN, out_refs[:n_main]),
              (wd_ref, IN_SEGS_MID, out_refs[n_main:n_main + n_mid]),
              (wg_ref, IN_SEGS[-1:], out_refs[n_main + n_mid:]))
    for w_ref, segs, refs in groups:
        total = sum(s[1] for s in segs)
        for a in range(0, total, 1024):
            b = min(a + 1024, total)
            y = _dot(h, w_ref[:, a:b])
            c0 = 0
            for out_ref, (_, width, dtype) in zip(refs, segs):
                lo, hi = max(a, c0), min(b, c0 + width)
                if lo < hi:
                    out_ref[:, lo - c0:hi - c0] = y[:, lo - a:hi - a].astype(dtype)
                c0 += width


def _in_proj(x2, g, w_main, w_mid, w_gate, layer, tm=512):
    t = x2.shape[0]
    out_shape = tuple(jax.ShapeDtypeStruct((t, wd), dt) for _, wd, dt in IN_SEGS)
    out_specs = tuple(pl.BlockSpec((tm, wd), lambda i: (i, 0)) for _, wd, _ in IN_SEGS)
    return pl.pallas_call(
        _in_proj_kernel,
        grid=(t // tm,),
        in_specs=[
            pl.BlockSpec((tm, D_MODEL), lambda i: (i, 0)),
            _layer_spec(layer, (1, D_MODEL)),
            _layer_spec(layer, (D_MODEL, IN_MAIN)),
            _layer_spec(layer, (D_MODEL, IN_MID)),
            _layer_spec(layer, (D_MODEL, IN_GATE)),
        ],
        out_specs=out_specs,
        out_shape=out_shape,
        compiler_params=_params(1),
        name="in_proj",
    )(x2, g, w_main, w_mid, w_gate)


def _pool_kernel(u_ref, wp_ref, ps_ref, o_ref):
    s_len = u_ref.shape[0]
    t = lax.broadcasted_iota(jnp.int32, (s_len, POOL_GROUP_DIM), 0)
    for g, w in enumerate(POOL_WINDOWS):
        cols = slice(g * POOL_GROUP_DIM, (g + 1) * POOL_GROUP_DIM)
        u = u_ref[:, cols]
        s = u
        sh = 1
        while sh < w:
            s = s + jnp.where(t >= sh, pltpu.roll(s, sh, axis=0), 0.0)
            sh *= 2
        cnt = jnp.minimum(t + 1, w).astype(F32)
        diff = s / cnt - u
        y = _dot(diff.astype(BF16), wp_ref[g]) * ps_ref[:, cols]
        o_ref[:, cols] = y.astype(BF16)


def _pool(u, wp, ps, layer, batch, seq):
    t = u.shape[0]
    return pl.pallas_call(
        _pool_kernel,
        grid=(batch,),
        in_specs=[
            pl.BlockSpec((seq, POOL_WIDTH), lambda b: (b, 0)),
            _layer_spec(layer, wp.shape[1:]),
            _layer_spec(layer, (1, POOL_WIDTH)),
        ],
        out_specs=pl.BlockSpec((seq, POOL_WIDTH), lambda b: (b, 0)),
        out_shape=jax.ShapeDtypeStruct((t, POOL_WIDTH), BF16),
        compiler_params=_params(1),
        name="pool",
    )(u, wp, ps)


def _gla_kernel(q_ref, k_ref, v_ref, r_ref, misc_ref, wa2_ref, ba_ref, gn_ref, o_ref):
    s_len = q_ref.shape[0]
    c = GLA_CHUNK
    sc = GLA_SUPER
    row = lax.broadcasted_iota(jnp.int32, (sc, sc), 0)
    col = lax.broadcasted_iota(jnp.int32, (sc, sc), 1)
    same_chunk = (row // c) == (col // c)
    tril = same_chunk & (row >= col)
    tri_bf = tril.astype(BF16)
    lane = lax.broadcasted_iota(jnp.int32, (1, 2 * GLA_DK), 1)
    head_lanes = [(lane >= hh * GLA_DK) & (lane < (hh + 1) * GLA_DK) for hh in range(2)]
    vcols = [slice(hh * GLA_DV, (hh + 1) * GLA_DV) for hh in range(2)]
    blk = lax.broadcasted_iota(jnp.int32, (sc, 2 * GLA_DK), 0) // c
    nsub = sc // c

    def spread(x):
        return jnp.concatenate(
            [jnp.where(blk == j, x, 0.0).astype(BF16) for j in range(nsub)], axis=1)

    blocks = [slice(n * sc, (n + 1) * sc) for n in range(s_len // sc)]

    z = _dot(misc_ref[...], wa2_ref[...]) + ba_ref[...]
    la = (jnp.minimum(z, 0.0) - jnp.log1p(jnp.exp(-jnp.abs(z)))) * (1.0 / GLA_GATE_TAU)
    la3 = jnp.concatenate(_split3(la), axis=1)
    cum3 = [_dot(tri_bf, la3[rows]) for rows in blocks]

    q_dec, k_inv, k_end, last = [], [], [], []
    for rows, c3 in zip(blocks, cum3):
        cum = c3[:, :LANES] + c3[:, LANES:2 * LANES] + c3[:, 2 * LANES:]
        ends = [cum[(j + 1) * c - 1:(j + 1) * c, :] for j in range(nsub)]
        tot = jnp.concatenate([jnp.broadcast_to(r, (c, 2 * GLA_DK)) for r in ends], axis=0)
        qf = q_ref[rows, :].astype(F32) * (GLA_DK ** -0.5)
        kf = k_ref[rows, :].astype(F32)
        q_dec.append(qf * jnp.exp(cum))
        k_inv.append((kf * jnp.exp(-cum)).astype(BF16))
        k_end.append(kf * jnp.exp(tot - cum))
        last.append(ends)

    upd = [sum(_dot_tn(v_ref[rows, vcols[hh]], spread(jnp.where(head_lanes[hh], ke, 0.0)))
               for hh in range(2)) for rows, ke in zip(blocks, k_end)]
    att = [[jnp.where(tril, _dot_nt(jnp.where(head_lanes[hh], qd, 0.0).astype(BF16), ki),
                      0.0).astype(BF16) for hh in range(2)] for qd, ki in zip(q_dec, k_inv)]

    st = jnp.zeros((GLA_DV, 2 * GLA_DK), F32)
    st_cat = []
    for ends, u in zip(last, upd):
        sts = []
        for j in range(nsub):
            sts.append(st.astype(BF16))
            st = st * jnp.exp(ends[j]) + u[:, j * LANES:(j + 1) * LANES]
        st_cat.append(jnp.concatenate(sts, axis=1))

    for rows, qd, a, sc_n in zip(blocks, q_dec, att, st_cat):
        for hh in range(2):
            o = _dot(a[hh], v_ref[rows, vcols[hh]])
            o = o + _dot_nt(spread(jnp.where(head_lanes[hh], qd, 0.0)), sc_n)
            o = _rms(o, gn_ref[:, vcols[hh]])
            rr = r_ref[rows, vcols[hh]].astype(F32)
            o_ref[rows, vcols[hh]] = (o * (rr * _sigmoid(rr))).astype(BF16)


def _gla(q, k, v, r, misc, wa2, ba, gn, layer, batch, seq):
    t = q.shape[0]
    kw = 2 * GLA_DK
    vw = 2 * GLA_DV
    return pl.pallas_call(
        _gla_kernel,
        grid=(batch, GLA_HEADS // 2),
        in_specs=[
            pl.BlockSpec((seq, kw), lambda b, p: (b, p)),
            pl.BlockSpec((seq, kw), lambda b, p: (b, p)),
            pl.BlockSpec((seq, vw), lambda b, p: (b, p)),
            pl.BlockSpec((seq, vw), lambda b, p: (b, p)),
            pl.BlockSpec((seq, LANES), lambda b, p: (b, 0)),
            pl.BlockSpec((None, LANES, kw), lambda b, p: (layer, 0, p)),
            pl.BlockSpec((None, 1, kw), lambda b, p: (layer, 0, p)),
            pl.BlockSpec((None, 1, vw), lambda b, p: (layer, 0, p)),
        ],
        out_specs=pl.BlockSpec((seq, vw), lambda b, p: (b, p)),
        out_shape=jax.ShapeDtypeStruct((t, GLA_VAL_WIDTH), BF16),
        compiler_params=_params(2),
        name="gla",
    )(q, k, v, r, misc, wa2, ba, gn)


def _mla_prep_kernel(cq_ref, ckv_ref, misc_ref, c_ref, s_ref, qn_ref, kvn_ref,
                     wq_ref, wk_ref, wv_ref, q_out, k_out, v_out):
    c_tab = c_ref[...]
    s_tab = s_ref[...]
    scale = MLA_QK ** -0.5 * LOG2E
    cqn = _rms(cq_ref[...].astype(F32), qn_ref[...]).astype(BF16)
    qf = _dot(cqn, wq_ref[...])
    ckvn = _rms(ckv_ref[...].astype(F32), kvn_ref[...]).astype(BF16)
    kf = _dot(ckvn, wk_ref[...])
    v_out[...] = _dot(ckvn, wv_ref[...]).astype(BF16)
    lane = lax.broadcasted_iota(jnp.int32, (1, HEAD_PAD), 1)
    rope_lane = ((lane >= 32) & (lane < 48)) | ((lane >= 96) & (lane < 112))
    kr = misc_ref[...].astype(F32)
    kr = jnp.where(rope_lane, kr * c_tab + pltpu.roll(kr, 64, axis=1) * s_tab, 0.0)
    for h in range(MLA_HEADS):
        cols = slice(h * HEAD_PAD, (h + 1) * HEAD_PAD)
        x = qf[:, cols]
        q_out[:, cols] = ((x * c_tab + pltpu.roll(x, 64, axis=1) * s_tab) * scale).astype(BF16)
        k_out[:, cols] = (kf[:, cols] + kr).astype(BF16)


def _mla_prep(cq, ckv, misc, c_tab, s_tab, qn, kvn, wq, wk, wv, layer, tm=512):
    t = cq.shape[0]
    hw = MLA_HEADS * HEAD_PAD
    row = lambda w: pl.BlockSpec((tm, w), lambda i: (i, 0))
    return pl.pallas_call(
        _mla_prep_kernel,
        grid=(t // tm,),
        in_specs=[
            row(MLA_Q_RANK), row(MLA_KV_RANK), row(LANES), row(HEAD_PAD), row(HEAD_PAD),
            _layer_spec(layer, (1, MLA_Q_RANK)), _layer_spec(layer, (1, MLA_KV_RANK)),
            _layer_spec(layer, (MLA_Q_RANK, hw)), _layer_spec(layer, (MLA_KV_RANK, hw)),
            _layer_spec(layer, (MLA_KV_RANK, MLA_VAL_WIDTH)),
        ],
        out_specs=(row(hw), row(hw), row(MLA_VAL_WIDTH)),
        out_shape=(jax.ShapeDtypeStruct((t, hw), BF16), jax.ShapeDtypeStruct((t, hw), BF16),
                   jax.ShapeDtypeStruct((t, MLA_VAL_WIDTH), BF16)),
        compiler_params=_params(1),
        name="mla_prep",
    )(cq, ckv, misc, c_tab, s_tab, qn, kvn, wq, wk, wv)


ATT_TQ = 512
ATT_HALF = ATT_TQ // 2


def _attn_scores(q_ref, k_ref, qi, cols):
    tq, hf = ATT_TQ, ATT_HALF
    r0 = qi * tq
    q = q_ref[r0:r0 + tq, cols]
    k_d = k_ref[r0:r0 + tq, cols]
    d0 = _dot_nt(k_d[:hf], q[:hf])
    d1 = _dot_nt(k_d, q[hf:])
    s_off = _dot_nt(k_ref[0:r0, cols], q) if qi > 0 else None
    return d0, d1, s_off


def _attn_finish(vt, qi, scores):
    tq, hf = ATT_TQ, ATT_HALF
    r0 = qi * tq
    neg = jnp.finfo(F32).min
    d0, d1, s_off = scores
    key0 = lax.broadcasted_iota(jnp.int32, (hf, hf), 0)
    qry0 = lax.broadcasted_iota(jnp.int32, (hf, hf), 1)
    key1 = lax.broadcasted_iota(jnp.int32, (tq, hf), 0)
    qry1 = lax.broadcasted_iota(jnp.int32, (tq, hf), 1)
    d0 = jnp.where(key0 <= qry0, d0, neg)
    d1 = jnp.where(key1 <= qry1 + hf, d1, neg)
    m0 = jnp.max(d0, axis=0, keepdims=True)
    m1 = jnp.max(d1, axis=0, keepdims=True)
    if s_off is not None:
        m0 = jnp.maximum(m0, jnp.max(s_off[:, :hf], axis=0, keepdims=True))
        m1 = jnp.maximum(m1, jnp.max(s_off[:, hf:], axis=0, keepdims=True))
        p_off0 = jnp.exp2(s_off[:, :hf] - m0)
        p_off1 = jnp.exp2(s_off[:, hf:] - m1)
        p_off = jnp.concatenate([p_off0.astype(BF16), p_off1.astype(BF16)], axis=1)
        o_off = _dot(vt[:, 0:r0], p_off)
    p0 = jnp.exp2(d0 - m0)
    p1 = jnp.exp2(d1 - m1)
    l0 = jnp.sum(p0, axis=0, keepdims=True)
    l1 = jnp.sum(p1, axis=0, keepdims=True)
    o0 = _dot(vt[:, r0:r0 + hf], p0.astype(BF16))
    o1 = _dot(vt[:, r0:r0 + tq], p1.astype(BF16))
    if s_off is not None:
        l0 = l0 + jnp.sum(p_off0, axis=0, keepdims=True)
        l1 = l1 + jnp.sum(p_off1, axis=0, keepdims=True)
        o0 = o0 + o_off[:, :hf]
        o1 = o1 + o_off[:, hf:]
    return o0 / l0, o1 / l1


def _attn_kernel(q_ref, k_ref, v_ref, o_ref):
    tq, hf = ATT_TQ, ATT_HALF
    head_cols = (slice(0, HEAD_PAD), slice(HEAD_PAD, 2 * HEAD_PAD))
    vt_pair = v_ref[...].astype(F32).T
    vrow = lax.broadcasted_iota(jnp.int32, vt_pair.shape, 0)
    vt_head = (jnp.where(vrow < MLA_V, vt_pair, 0.0).astype(BF16),
               jnp.where(vrow >= MLA_V, vt_pair, 0.0).astype(BF16))
    units = [(qi, hh) for qi in reversed(range(q_ref.shape[0] // tq)) for hh in range(2)]
    scores = _attn_scores(q_ref, k_ref, units[0][0], head_cols[units[0][1]])
    prev = None
    for i, (qi, hh) in enumerate(units):
        nxt = None
        if i + 1 < len(units):
            nxt = _attn_scores(q_ref, k_ref, units[i + 1][0], head_cols[units[i + 1][1]])
        cur = _attn_finish(vt_head[hh], qi, scores)
        scores = nxt
        if hh == 0:
            prev = cur
        else:
            o_ref[qi * tq:qi * tq + hf, :] = (prev[0] + cur[0]).T.astype(BF16)
            o_ref[qi * tq + hf:(qi + 1) * tq, :] = (prev[1] + cur[1]).T.astype(BF16)


def _attention(q, k, v, batch, seq):
    t = q.shape[0]
    pw = 2 * HEAD_PAD
    pair = pl.BlockSpec((seq, pw), lambda b, p: (b, p))
    return pl.pallas_call(
        _attn_kernel,
        grid=(batch, MLA_HEADS // 2),
        in_specs=[pair, pair, pl.BlockSpec((seq, HEAD_PAD), lambda b, p: (b, p))],
        out_specs=pl.BlockSpec((seq, HEAD_PAD), lambda b, p: (b, p)),
        out_shape=jax.ShapeDtypeStruct((t, MLA_VAL_WIDTH), BF16),
        compiler_params=_params(2),
        name="attention",
    )(q, k, v)


FFN_CHUNKS = ((0, 1024), (1024, 2048), (2048, D_FF))
MERGE_FFN_VMEM = 60 * 1024 * 1024
MERGE_FFN_SUB = 2


def _merge_ffn_kernel(x_ref, pa_ref, gb_ref, mc_ref, gl_ref, wa_ref, wb_ref, wc_ref, wo_ref,
                      n_mix_ref, n1_ref, wg_ref, wu_ref, wd_ref, n2_ref, o_ref):
    tm = x_ref.shape[0]
    sub = tm // MERGE_FFN_SUB
    groups = [slice(s * sub, (s + 1) * sub) for s in range(MERGE_FFN_SUB)]
    branches = ((pa_ref, wa_ref), (gb_ref, wb_ref), (mc_ref, wc_ref))

    merged = []
    for rows in groups:
        m = None
        for i, (b_ref, w_ref) in enumerate(branches):
            y = _dot(b_ref[rows, :], w_ref[...])
            gate = _sigmoid(gl_ref[rows, i * D_MODEL:(i + 1) * D_MODEL].astype(F32))
            m = gate * y if m is None else m + gate * y
        merged.append(m.astype(BF16))
    xs, hs = [], []
    for rows, m in zip(groups, merged):
        x = x_ref[rows, :] + _rms(_dot(m, wo_ref[...]), n_mix_ref[...])
        xs.append(x)
        hs.append(_rms(x, n1_ref[...]).astype(BF16))
    accs = [None] * len(groups)
    for a, b in FFN_CHUNKS:
        gu = [(_dot(h, wg_ref[:, a:b]), _dot(h, wu_ref[:, a:b])) for h in hs]
        for s, (g, u) in enumerate(gu):
            d = _dot((g * _sigmoid(g) * u).astype(BF16), wd_ref[a:b, :])
            accs[s] = d if accs[s] is None else accs[s] + d
    for rows, x, acc in zip(groups, xs, accs):
        o_ref[rows, :] = x + _rms(acc, n2_ref[...])


def _merge_ffn(x2, pa, gb, mc, glog, wa, wb, wc, wo, n_mix, n1, wgu, wd, n2, layer, tm=512):
    t = x2.shape[0]
    row = lambda w: pl.BlockSpec((tm, w), lambda i: (i, 0))
    return pl.pallas_call(
        _merge_ffn_kernel,
        grid=(t // tm,),
        in_specs=[
            row(D_MODEL), row(POOL_WIDTH), row(GLA_VAL_WIDTH), row(MLA_VAL_WIDTH),
            row(N_BRANCH * D_MODEL),
            _layer_spec(layer, (POOL_WIDTH, D_MODEL)), _layer_spec(layer, (GLA_VAL_WIDTH, D_MODEL)),
            _layer_spec(layer, (MLA_VAL_WIDTH, D_MODEL)), _layer_spec(layer, (D_MODEL, D_MODEL)),
            _layer_spec(layer, (1, D_MODEL)), _layer_spec(layer, (1, D_MODEL)),
            _layer_spec(layer, (D_MODEL, D_FF), col=0), _layer_spec(layer, (D_MODEL, D_FF), col=1),
            _layer_spec(layer, (D_FF, D_MODEL)), _layer_spec(layer, (1, D_MODEL)),
        ],
        out_specs=row(D_MODEL),
        out_shape=jax.ShapeDtypeStruct((t, D_MODEL), F32),
        compiler_params=pltpu.CompilerParams(
            dimension_semantics=("arbitrary",), vmem_limit_bytes=MERGE_FFN_VMEM),
        name="merge_ffn",
    )(x2, pa, gb, mc, glog, wa, wb, wc, wo, n_mix, n1, wgu, wgu, wd, n2)


def _prep_weights(w_in, w_gla_a2, w_mla_uq, w_mla_ukv):
    sizes = (POOL_WIDTH, GLA_KEY_WIDTH, GLA_KEY_WIDTH, GLA_VAL_WIDTH, GLA_VAL_WIDTH,
             GLA_GATE_RANK, MLA_Q_RANK, MLA_KV_RANK, MLA_ROPE, N_BRANCH * D_MODEL)
    offs = [0]
    for s in sizes:
        offs.append(offs[-1] + s)
    assert offs[5] == IN_MAIN
    w_main = w_in[:, :, :IN_MAIN].astype(BF16)
    w_a1, w_cq, w_ckv, w_kr, w_gate = (w_in[:, :, offs[i]:offs[i + 1]] for i in range(5, 10))
    w_gate = w_gate.astype(BF16)
    d = w_in.shape[0]
    z16 = jnp.zeros((d, D_MODEL, 16), w_in.dtype)
    z48 = jnp.zeros((d, D_MODEL, 48), w_in.dtype)
    w_mid = jnp.concatenate(
        [w_cq, w_ckv, w_a1, z16, w_kr[..., :16], z48, w_kr[..., 16:], z16], axis=-1).astype(BF16)

    wa2 = jnp.pad(w_gla_a2, ((0, 0), (0, LANES - GLA_GATE_RANK), (0, 0))).astype(BF16)

    uq = w_mla_uq.reshape(d, MLA_Q_RANK, MLA_HEADS, MLA_QK)
    zq = jnp.zeros((d, MLA_Q_RANK, MLA_HEADS, 16), w_mla_uq.dtype)
    wq = jnp.concatenate(
        [uq[..., 0:32], uq[..., 64:80], zq, uq[..., 32:64], uq[..., 80:96], zq],
        axis=-1).reshape(d, MLA_Q_RANK, MLA_HEADS * HEAD_PAD).astype(BF16)

    ukv = w_mla_ukv.reshape(d, MLA_KV_RANK, MLA_HEADS, MLA_NOPE + MLA_V)
    zk = jnp.zeros((d, MLA_KV_RANK, MLA_HEADS, 32), w_mla_ukv.dtype)
    wk = jnp.concatenate(
        [ukv[..., 0:32], zk, ukv[..., 32:64], zk],
        axis=-1).reshape(d, MLA_KV_RANK, MLA_HEADS * HEAD_PAD).astype(BF16)
    wv = ukv[..., MLA_NOPE:].reshape(d, MLA_KV_RANK, MLA_VAL_WIDTH).astype(BF16)
    return w_main, w_mid, w_gate, wa2, wq, wk, wv


def kernel(x, positions, norm_pre_mix, norm_post_mix, norm_pre_ffn, norm_post_ffn, w_in, w_pool,
           pool_scale, w_a, w_gla_a2, b_gla_a, gla_norm, w_b, mla_q_norm, w_mla_uq, mla_kv_norm,
           w_mla_ukv, w_c, w_o, w_ffn_gu, w_ffn_down):
    batch, seq, d_model = x.shape
    t = batch * seq
    x2 = x.reshape(t, d_model)
    c_tab, s_tab = _rope_tables(positions)
    w_main, w_mid, w_gate, wa2, wq, wk, wv = _prep_weights(w_in, w_gla_a2, w_mla_uq, w_mla_ukv)
    w_pool_b = w_pool.astype(BF16)
    w_a_b, w_b_b, w_c_b, w_o_b = (w.astype(BF16) for w in (w_a, w_b, w_c, w_o))
    w_gu_b = w_ffn_gu.astype(BF16)
    w_d_b = w_ffn_down.astype(BF16)
    vec = lambda a: a.reshape(DEPTH, 1, -1)
    n_pre_mix, n_post_mix, n_pre_ffn, n_post_ffn = (
        vec(a) for a in (norm_pre_mix, norm_post_mix, norm_pre_ffn, norm_post_ffn))
    pool_scale, b_gla_a, gla_norm, mla_q_norm, mla_kv_norm = (
        vec(a) for a in (pool_scale, b_gla_a, gla_norm, mla_q_norm, mla_kv_norm))
    for l in range(DEPTH):
        u_pool, gq, gk, gv, gr, cq, ckv, misc, glog = _in_proj(
            x2, n_pre_mix, w_main, w_mid, w_gate, l)
        pa = _pool(u_pool, w_pool_b, pool_scale, l, batch, seq)
        gb = _gla(gq, gk, gv, gr, misc, wa2, b_gla_a, gla_norm, l, batch, seq)
        mq, mk, mv = _mla_prep(cq, ckv, misc, c_tab, s_tab, mla_q_norm, mla_kv_norm,
                               wq, wk, wv, l)
        mc = _attention(mq, mk, mv, batch, seq)
        x2 = _merge_ffn(x2, pa, gb, mc, glog, w_a_b, w_b_b, w_c_b, w_o_b, n_post_mix,
                        n_pre_ffn, w_gu_b, w_d_b, n_post_ffn, l)
    return x2.reshape(batch, seq, d_model)
```

```python
import functools

import jax
import jax.numpy as jnp
import numpy as np
from jax import lax
from jax.experimental import pallas as pl
from jax.experimental.pallas import tpu as pltpu

F32 = jnp.float32
BF16 = jnp.bfloat16

D_MODEL = 1024
DEPTH = 4
POOL_WIDTH = 512
POOL_WINDOWS = (2, 4, 8, 16)
POOL_GROUP_DIM = 128
GLA_HEADS = 4
GLA_DK = 64
GLA_DV = 128
GLA_KEY_WIDTH = GLA_HEADS * GLA_DK
GLA_VAL_WIDTH = GLA_HEADS * GLA_DV
GLA_GATE_RANK = 16
GLA_GATE_TAU = 16.0
GLA_CHUNK = 64
GLA_SUPER = 256
MLA_HEADS = 8
MLA_Q_RANK = 384
MLA_KV_RANK = 256
MLA_NOPE = 64
MLA_ROPE = 32
MLA_V = 64
MLA_QK = MLA_NOPE + MLA_ROPE
MLA_VAL_WIDTH = MLA_HEADS * MLA_V
ROPE_BASE = 10000.0
N_BRANCH = 3
D_FF = 2816
EPS = 1e-6
LOG2E = 1.4426950408889634

LANES = 128
HEAD_PAD = 128
VMEM_LIMIT = 56 * 1024 * 1024

IN_MAIN = 2048
IN_MID = MLA_Q_RANK + MLA_KV_RANK + LANES
IN_GATE = N_BRANCH * D_MODEL


def _dot(a, b):
    return jnp.dot(a, b, preferred_element_type=F32)


def _dot_nt(a, b):
    return lax.dot_general(a, b, (((1,), (1,)), ((), ())), preferred_element_type=F32)


def _dot_tn(a, b):
    return lax.dot_general(a, b, (((0,), (0,)), ((), ())), preferred_element_type=F32)


def _rms(x, g):
    return x * lax.rsqrt(jnp.mean(x * x, axis=-1, keepdims=True) + EPS) * g


def _sigmoid(x):
    return 0.5 * jnp.tanh(0.5 * x) + 0.5


def _params(n_axes):
    return pltpu.CompilerParams(
        dimension_semantics=("arbitrary",) * n_axes, vmem_limit_bytes=VMEM_LIMIT)


def _layer_spec(layer, tail, col=0):
    idx = (layer,) + (0,) * (len(tail) - 1) + (col,)
    return pl.BlockSpec((None,) + tuple(tail), lambda *_: idx, pipeline_mode=pl.Buffered(1))


ROPE_HALF = MLA_ROPE // 2
TOK_PER_ROW = LANES // ROPE_HALF


def _split3(x):
    hi = x.astype(BF16)
    r1 = x - hi.astype(F32)
    mid = r1.astype(BF16)
    lo = (r1 - mid.astype(F32)).astype(BF16)
    return hi, mid, lo


def _rope_kernel(pos_ref, invf_ref, ec_ref, es_ref, base_ref, c_ref, s_ref):
    ang = pos_ref[...].astype(F32) * invf_ref[...]
    cos3 = _split3(jnp.cos(ang))
    sin3 = _split3(jnp.sin(ang))
    for j in range(TOK_PER_ROW):
        ec, es = ec_ref[j], es_ref[j]
        c_ref[:, j, :] = sum(_dot(t, ec) for t in cos3) + base_ref[...]
        s_ref[:, j, :] = sum(_dot(t, es) for t in sin3)


def _rope_tables(positions):
    t = positions.size
    inv_freq = ROPE_BASE ** (-jnp.arange(0, MLA_ROPE, 2, dtype=F32) / MLA_ROPE)
    rows = t // TOK_PER_ROW
    pos_d = jnp.repeat(positions.reshape(t), ROPE_HALF).reshape(rows, LANES)
    invf_d = jnp.tile(inv_freq, TOK_PER_ROW).reshape(1, LANES)
    ec = np.zeros((TOK_PER_ROW, LANES, HEAD_PAD), np.float32)
    es = np.zeros((TOK_PER_ROW, LANES, HEAD_PAD), np.float32)
    for j in range(TOK_PER_ROW):
        for i in range(ROPE_HALF):
            ec[j, j * ROPE_HALF + i, 32 + i] = 1.0
            ec[j, j * ROPE_HALF + i, 96 + i] = 1.0
            es[j, j * ROPE_HALF + i, 32 + i] = -1.0
            es[j, j * ROPE_HALF + i, 96 + i] = 1.0
    base = np.zeros((1, HEAD_PAD), np.float32)
    base[0, 0:32] = 1.0
    base[0, 64:96] = 1.0
    c3, s3 = pl.pallas_call(
        _rope_kernel,
        out_shape=(jax.ShapeDtypeStruct((rows, TOK_PER_ROW, HEAD_PAD), F32),) * 2,
        compiler_params=pltpu.CompilerParams(vmem_limit_bytes=VMEM_LIMIT),
        name="rope_tables",
    )(pos_d, invf_d, jnp.asarray(ec, BF16), jnp.asarray(es, BF16), jnp.asarray(base))
    return c3.reshape(t, HEAD_PAD), s3.reshape(t, HEAD_PAD)


POOL_HALO = 16


def _front_kernel(x_ref, g_ref, wm_ref, wd_ref, wg_ref, c_ref, s_ref, qn_ref, kvn_ref,
                  wq_ref, wk_ref, wv_ref, wp_ref, ps_ref,
                  gq_ref, gk_ref, gv_ref, gr_ref, misc_ref, glog_ref, pa_ref, q_out, k_out, v_out,
                  halo_ref, *, tiles_per_seq):
    tm = x_ref.shape[0]
    tile_in_seq = pl.program_id(0) % tiles_per_seq

    @pl.when(tile_in_seq == 0)
    def _():
        halo_ref[...] = jnp.zeros_like(halo_ref)

    h = _rms(x_ref[...], g_ref[...]).astype(BF16)

    y_mid = _dot(h, wd_ref[...])
    cq = y_mid[:, :MLA_Q_RANK]
    ckv = y_mid[:, MLA_Q_RANK:MLA_Q_RANK + MLA_KV_RANK]
    misc = y_mid[:, MLA_Q_RANK + MLA_KV_RANK:]
    misc_ref[...] = misc.astype(BF16)

    y0 = _dot(h, wm_ref[:, :2 * POOL_WIDTH])
    u = y0[:, :POOL_WIDTH]
    gq_ref[...] = y0[:, POOL_WIDTH:POOL_WIDTH + GLA_KEY_WIDTH].astype(BF16)
    gk_ref[...] = y0[:, POOL_WIDTH + GLA_KEY_WIDTH:].astype(BF16)
    y1 = _dot(h, wm_ref[:, 2 * POOL_WIDTH:])
    gv_ref[...] = y1[:, :GLA_VAL_WIDTH].astype(BF16)
    gr_ref[...] = y1[:, GLA_VAL_WIDTH:].astype(BF16)

    cqn = _rms(cq, qn_ref[...]).astype(BF16)
    qf = _dot(cqn, wq_ref[...])
    ckvn = _rms(ckv, kvn_ref[...]).astype(BF16)
    kf = _dot(ckvn, wk_ref[...])
    v_out[...] = _dot(ckvn, wv_ref[...]).astype(BF16)

    ext = jnp.concatenate([halo_ref[...], u], axis=0)
    halo_ref[...] = u[tm - POOL_HALO:, :]
    pos = tile_in_seq * tm + lax.broadcasted_iota(jnp.int32, (tm, POOL_GROUP_DIM), 0)
    for g, w in enumerate(POOL_WINDOWS):
        cols = slice(g * POOL_GROUP_DIM, (g + 1) * POOL_GROUP_DIM)
        s = ext[:, cols]
        sh = 1
        while sh < w:
            s = s + pltpu.roll(s, sh, axis=0)
            sh *= 2
        cnt = jnp.minimum(pos + 1, w).astype(F32)
        diff = s[POOL_HALO:] / cnt - u[:, cols]
        pa_ref[:, cols] = (_dot(diff.astype(BF16), wp_ref[g]) * ps_ref[:, cols]).astype(BF16)

    for a in range(0, IN_GATE, 1024):
        glog_ref[:, a:a + 1024] = _dot(h, wg_ref[:, a:a + 1024]).astype(BF16)

    c_tab = c_ref[...]
    s_tab = s_ref[...]
    scale = MLA_QK ** -0.5 * LOG2E
    lane = lax.broadcasted_iota(jnp.int32, (1, HEAD_PAD), 1)
    rope_lane = ((lane >= 32) & (lane < 48)) | ((lane >= 96) & (lane < 112))
    kr = jnp.where(rope_lane, misc * c_tab + pltpu.roll(misc, 64, axis=1) * s_tab, 0.0)
    for hd in range(MLA_HEADS):
        cols = slice(hd * HEAD_PAD, (hd + 1) * HEAD_PAD)
        xq = qf[:, cols]
        q_out[:, cols] = ((xq * c_tab + pltpu.roll(xq, 64, axis=1) * s_tab) * scale).astype(BF16)
        k_out[:, cols] = (kf[:, cols] + kr).astype(BF16)


def _front(x2, g, w_main, w_mid, w_gate, c_tab, s_tab, qn, kvn, wq, wk, wv, wp, ps, layer, seq,
           tm=512):
    t = x2.shape[0]
    hw = MLA_HEADS * HEAD_PAD
    row = lambda w: pl.BlockSpec((tm, w), lambda i: (i, 0))
    outs = ((GLA_KEY_WIDTH, BF16), (GLA_KEY_WIDTH, BF16), (GLA_VAL_WIDTH, BF16),
            (GLA_VAL_WIDTH, BF16), (LANES, BF16), (IN_GATE, BF16), (POOL_WIDTH, BF16),
            (hw, BF16), (hw, BF16), (MLA_VAL_WIDTH, BF16))
    return pl.pallas_call(
        functools.partial(_front_kernel, tiles_per_seq=seq // tm),
        grid=(t // tm,),
        in_specs=[
            row(D_MODEL),
            _layer_spec(layer, (1, D_MODEL)),
            _layer_spec(layer, (D_MODEL, IN_MAIN)),
            _layer_spec(layer, (D_MODEL, IN_MID)),
            _layer_spec(layer, (D_MODEL, IN_GATE)),
            row(HEAD_PAD), row(HEAD_PAD),
            _layer_spec(layer, (1, MLA_Q_RANK)), _layer_spec(layer, (1, MLA_KV_RANK)),
            _layer_spec(layer, (MLA_Q_RANK, hw)), _layer_spec(layer, (MLA_KV_RANK, hw)),
            _layer_spec(layer, (MLA_KV_RANK, MLA_VAL_WIDTH)),
            _layer_spec(layer, wp.shape[1:]),
            _layer_spec(layer, (1, POOL_WIDTH)),
        ],
        out_specs=tuple(row(w) for w, _ in outs),
        out_shape=tuple(jax.ShapeDtypeStruct((t, w), dt) for w, dt in outs),
        scratch_shapes=[pltpu.VMEM((POOL_HALO, POOL_WIDTH), F32)],
        compiler_params=_params(1),
        name="front",
    )(x2, g, w_main, w_mid, w_gate, c_tab, s_tab, qn, kvn, wq, wk, wv, wp, ps)


def _gla_kernel(q_ref, k_ref, v_ref, r_ref, misc_ref, wa2_ref, ba_ref, gn_ref, o_ref):
    s_len = q_ref.shape[0]
    c = GLA_CHUNK
    sc = GLA_SUPER
    row = lax.broadcasted_iota(jnp.int32, (sc, sc), 0)
    col = lax.broadcasted_iota(jnp.int32, (sc, sc), 1)
    same_chunk = (row // c) == (col // c)
    tril = same_chunk & (row >= col)
    tri_bf = tril.astype(BF16)
    lane = lax.broadcasted_iota(jnp.int32, (1, 2 * GLA_DK), 1)
    head_lanes = [(lane >= hh * GLA_DK) & (lane < (hh + 1) * GLA_DK) for hh in range(2)]
    vcols = [slice(hh * GLA_DV, (hh + 1) * GLA_DV) for hh in range(2)]
    blk = lax.broadcasted_iota(jnp.int32, (sc, 2 * GLA_DK), 0) // c
    nsub = sc // c

    def spread(x):
        return jnp.concatenate(
            [jnp.where(blk == j, x, 0.0).astype(BF16) for j in range(nsub)], axis=1)

    blocks = [slice(n * sc, (n + 1) * sc) for n in range(s_len // sc)]

    z = _dot(misc_ref[...], wa2_ref[...]) + ba_ref[...]
    la = (jnp.minimum(z, 0.0) - jnp.log1p(jnp.exp(-jnp.abs(z)))) * (1.0 / GLA_GATE_TAU)
    la3 = jnp.concatenate(_split3(la), axis=1)
    cum3 = [_dot(tri_bf, la3[rows]) for rows in blocks]

    q_dec, k_inv, k_end, last = [], [], [], []
    for rows, c3 in zip(blocks, cum3):
        cum = c3[:, :LANES] + c3[:, LANES:2 * LANES] + c3[:, 2 * LANES:]
        ends = [cum[(j + 1) * c - 1:(j + 1) * c, :] for j in range(nsub)]
        tot = jnp.concatenate([jnp.broadcast_to(r, (c, 2 * GLA_DK)) for r in ends], axis=0)
        qf = q_ref[rows, :].astype(F32) * (GLA_DK ** -0.5)
        kf = k_ref[rows, :].astype(F32)
        q_dec.append(qf * jnp.exp(cum))
        k_inv.append((kf * jnp.exp(-cum)).astype(BF16))
        k_end.append(kf * jnp.exp(tot - cum))
        last.append(ends)

    upd = [sum(_dot_tn(v_ref[rows, vcols[hh]], spread(jnp.where(head_lanes[hh], ke, 0.0)))
               for hh in range(2)) for rows, ke in zip(blocks, k_end)]
    att = [[jnp.where(tril, _dot_nt(jnp.where(head_lanes[hh], qd, 0.0).astype(BF16), ki),
                      0.0).astype(BF16) for hh in range(2)] for qd, ki in zip(q_dec, k_inv)]

    st = jnp.zeros((GLA_DV, 2 * GLA_DK), F32)
    st_cat = []
    for ends, u in zip(last, upd):
        sts = []
        for j in range(nsub):
            sts.append(st.astype(BF16))
            st = st * jnp.exp(ends[j]) + u[:, j * LANES:(j + 1) * LANES]
        st_cat.append(jnp.concatenate(sts, axis=1))

    for rows, qd, a, sc_n in zip(blocks, q_dec, att, st_cat):
        for hh in range(2):
            o = _dot(a[hh], v_ref[rows, vcols[hh]])
            o = o + _dot_nt(spread(jnp.where(head_lanes[hh], qd, 0.0)), sc_n)
            o = _rms(o, gn_ref[:, vcols[hh]])
            rr = r_ref[rows, vcols[hh]].astype(F32)
            o_ref[rows, vcols[hh]] = (o * (rr * _sigmoid(rr))).astype(BF16)


def _gla(q, k, v, r, misc, wa2, ba, gn, layer, batch, seq):
    t = q.shape[0]
    kw = 2 * GLA_DK
    vw = 2 * GLA_DV
    return pl.pallas_call(
        _gla_kernel,
        grid=(batch, GLA_HEADS // 2),
        in_specs=[
            pl.BlockSpec((seq, kw), lambda b, p: (b, p)),
            pl.BlockSpec((seq, kw), lambda b, p: (b, p)),
            pl.BlockSpec((seq, vw), lambda b, p: (b, p)),
            pl.BlockSpec((seq, vw), lambda b, p: (b, p)),
            pl.BlockSpec((seq, LANES), lambda b, p: (b, 0)),
            pl.BlockSpec((None, LANES, kw), lambda b, p: (layer, 0, p)),
            pl.BlockSpec((None, 1, kw), lambda b, p: (layer, 0, p)),
            pl.BlockSpec((None, 1, vw), lambda b, p: (layer, 0, p)),
        ],
        out_specs=pl.BlockSpec((seq, vw), lambda b, p: (b, p)),
        out_shape=jax.ShapeDtypeStruct((t, GLA_VAL_WIDTH), BF16),
        compiler_params=_params(2),
        name="gla",
    )(q, k, v, r, misc, wa2, ba, gn)


ATT_TQ = 512
ATT_HALF = ATT_TQ // 2
ATT_AHEAD = 2


def _attn_scores(q_ref, k_ref, qi, cols):
    tq, hf = ATT_TQ, ATT_HALF
    r0 = qi * tq
    q = q_ref[r0:r0 + tq, cols]
    k_d = k_ref[r0:r0 + tq, cols]
    d0 = _dot_nt(k_d[:hf], q[:hf])
    d1 = _dot_nt(k_d, q[hf:])
    s_off = _dot_nt(k_ref[0:r0, cols], q) if qi > 0 else None
    return d0, d1, s_off


def _attn_finish(vt, qi, scores):
    tq, hf = ATT_TQ, ATT_HALF
    r0 = qi * tq
    neg = jnp.finfo(F32).min
    d0, d1, s_off = scores
    key0 = lax.broadcasted_iota(jnp.int32, (hf, hf), 0)
    qry0 = lax.broadcasted_iota(jnp.int32, (hf, hf), 1)
    key1 = lax.broadcasted_iota(jnp.int32, (tq, hf), 0)
    qry1 = lax.broadcasted_iota(jnp.int32, (tq, hf), 1)
    d0 = jnp.where(key0 <= qry0, d0, neg)
    d1 = jnp.where(key1 <= qry1 + hf, d1, neg)
    m0 = jnp.max(d0, axis=0, keepdims=True)
    m1 = jnp.max(d1, axis=0, keepdims=True)
    if s_off is not None:
        m0 = jnp.maximum(m0, jnp.max(s_off[:, :hf], axis=0, keepdims=True))
        m1 = jnp.maximum(m1, jnp.max(s_off[:, hf:], axis=0, keepdims=True))
        p_off0 = jnp.exp2(s_off[:, :hf] - m0)
        p_off1 = jnp.exp2(s_off[:, hf:] - m1)
        p_off = jnp.concatenate([p_off0.astype(BF16), p_off1.astype(BF16)], axis=1)
        o_off = _dot(vt[:, 0:r0], p_off)
    p0 = jnp.exp2(d0 - m0)
    p1 = jnp.exp2(d1 - m1)
    l0 = jnp.sum(p0, axis=0, keepdims=True)
    l1 = jnp.sum(p1, axis=0, keepdims=True)
    o0 = _dot(vt[:, r0:r0 + hf], p0.astype(BF16))
    o1 = _dot(vt[:, r0:r0 + tq], p1.astype(BF16))
    if s_off is not None:
        l0 = l0 + jnp.sum(p_off0, axis=0, keepdims=True)
        l1 = l1 + jnp.sum(p_off1, axis=0, keepdims=True)
        o0 = o0 + o_off[:, :hf]
        o1 = o1 + o_off[:, hf:]
    return o0 / l0, o1 / l1


def _attn_kernel(q_ref, k_ref, v_ref, o_ref):
    tq, hf = ATT_TQ, ATT_HALF
    head_cols = (slice(0, HEAD_PAD), slice(HEAD_PAD, 2 * HEAD_PAD))
    vt_pair = v_ref[...].astype(F32).T
    vrow = lax.broadcasted_iota(jnp.int32, vt_pair.shape, 0)
    vt_head = (jnp.where(vrow < MLA_V, vt_pair, 0.0).astype(BF16),
               jnp.where(vrow >= MLA_V, vt_pair, 0.0).astype(BF16))
    units = [(qi, hh) for qi in reversed(range(q_ref.shape[0] // tq)) for hh in range(2)]
    pending = [_attn_scores(q_ref, k_ref, u[0], head_cols[u[1]]) for u in units[:ATT_AHEAD]]
    prev = None
    for i, (qi, hh) in enumerate(units):
        if i + ATT_AHEAD < len(units):
            u = units[i + ATT_AHEAD]
            pending.append(_attn_scores(q_ref, k_ref, u[0], head_cols[u[1]]))
        cur = _attn_finish(vt_head[hh], qi, pending.pop(0))
        if hh == 0:
            prev = cur
        else:
            o_ref[qi * tq:qi * tq + hf, :] = (prev[0] + cur[0]).T.astype(BF16)
            o_ref[qi * tq + hf:(qi + 1) * tq, :] = (prev[1] + cur[1]).T.astype(BF16)


def _attention(q, k, v, batch, seq):
    t = q.shape[0]
    pw = 2 * HEAD_PAD
    pair = pl.BlockSpec((seq, pw), lambda b, p: (b, p))
    return pl.pallas_call(
        _attn_kernel,
        grid=(batch, MLA_HEADS // 2),
        in_specs=[pair, pair, pl.BlockSpec((seq, HEAD_PAD), lambda b, p: (b, p))],
        out_specs=pl.BlockSpec((seq, HEAD_PAD), lambda b, p: (b, p)),
        out_shape=jax.ShapeDtypeStruct((t, MLA_VAL_WIDTH), BF16),
        compiler_params=_params(2),
        name="attention",
    )(q, k, v)


FFN_CHUNKS = ((0, 1024), (1024, 2048), (2048, D_FF))
MERGE_FFN_VMEM = 60 * 1024 * 1024
MERGE_FFN_SUB = 2


def _merge_ffn_kernel(x_ref, pa_ref, gb_ref, mc_ref, gl_ref, wa_ref, wb_ref, wc_ref, wo_ref,
                      n_mix_ref, n1_ref, wg_ref, wu_ref, wd_ref, n2_ref, o_ref):
    tm = x_ref.shape[0]
    sub = tm // MERGE_FFN_SUB
    groups = [slice(s * sub, (s + 1) * sub) for s in range(MERGE_FFN_SUB)]
    branches = ((pa_ref, wa_ref), (gb_ref, wb_ref), (mc_ref, wc_ref))

    merged = []
    for rows in groups:
        m = None
        for i, (b_ref, w_ref) in enumerate(branches):
            y = _dot(b_ref[rows, :], w_ref[...])
            gate = _sigmoid(gl_ref[rows, i * D_MODEL:(i + 1) * D_MODEL].astype(F32))
            m = gate * y if m is None else m + gate * y
        merged.append(m.astype(BF16))
    xs, hs = [], []
    for rows, m in zip(groups, merged):
        x = x_ref[rows, :] + _rms(_dot(m, wo_ref[...]), n_mix_ref[...])
        xs.append(x)
        hs.append(_rms(x, n1_ref[...]).astype(BF16))
    accs = [None] * len(groups)
    for a, b in FFN_CHUNKS:
        gu = [(_dot(h, wg_ref[:, a:b]), _dot(h, wu_ref[:, a:b])) for h in hs]
        for s, (g, u) in enumerate(gu):
            d = _dot((g * _sigmoid(g) * u).astype(BF16), wd_ref[a:b, :])
            accs[s] = d if accs[s] is None else accs[s] + d
    for rows, x, acc in zip(groups, xs, accs):
        o_ref[rows, :] = x + _rms(acc, n2_ref[...])


def _merge_ffn(x2, pa, gb, mc, glog, wa, wb, wc, wo, n_mix, n1, wgu, wd, n2, layer, tm=512):
    t = x2.shape[0]
    row = lambda w: pl.BlockSpec((tm, w), lambda i: (i, 0))
    return pl.pallas_call(
        _merge_ffn_kernel,
        grid=(t // tm,),
        in_specs=[
            row(D_MODEL), row(POOL_WIDTH), row(GLA_VAL_WIDTH), row(MLA_VAL_WIDTH),
            row(N_BRANCH * D_MODEL),
            _layer_spec(layer, (POOL_WIDTH, D_MODEL)), _layer_spec(layer, (GLA_VAL_WIDTH, D_MODEL)),
            _layer_spec(layer, (MLA_VAL_WIDTH, D_MODEL)), _layer_spec(layer, (D_MODEL, D_MODEL)),
            _layer_spec(layer, (1, D_MODEL)), _layer_spec(layer, (1, D_MODEL)),
            _layer_spec(layer, (D_MODEL, D_FF), col=0), _layer_spec(layer, (D_MODEL, D_FF), col=1),
            _layer_spec(layer, (D_FF, D_MODEL)), _layer_spec(layer, (1, D_MODEL)),
        ],
        out_specs=row(D_MODEL),
        out_shape=jax.ShapeDtypeStruct((t, D_MODEL), F32),
        compiler_params=pltpu.CompilerParams(
            dimension_semantics=("arbitrary",), vmem_limit_bytes=MERGE_FFN_VMEM),
        name="merge_ffn",
    )(x2, pa, gb, mc, glog, wa, wb, wc, wo, n_mix, n1, wgu, wgu, wd, n2)


def _prep_weights(w_in, w_gla_a2, w_mla_uq, w_mla_ukv):
    sizes = (POOL_WIDTH, GLA_KEY_WIDTH, GLA_KEY_WIDTH, GLA_VAL_WIDTH, GLA_VAL_WIDTH,
             GLA_GATE_RANK, MLA_Q_RANK, MLA_KV_RANK, MLA_ROPE, N_BRANCH * D_MODEL)
    offs = [0]
    for s in sizes:
        offs.append(offs[-1] + s)
    assert offs[5] == IN_MAIN
    w_main = w_in[:, :, :IN_MAIN].astype(BF16)
    w_a1, w_cq, w_ckv, w_kr, w_gate = (w_in[:, :, offs[i]:offs[i + 1]] for i in range(5, 10))
    w_gate = w_gate.astype(BF16)
    d = w_in.shape[0]
    z16 = jnp.zeros((d, D_MODEL, 16), w_in.dtype)
    z48 = jnp.zeros((d, D_MODEL, 48), w_in.dtype)
    w_mid = jnp.concatenate(
        [w_cq, w_ckv, w_a1, z16, w_kr[..., :16], z48, w_kr[..., 16:], z16], axis=-1).astype(BF16)

    wa2 = jnp.pad(w_gla_a2, ((0, 0), (0, LANES - GLA_GATE_RANK), (0, 0))).astype(BF16)

    uq = w_mla_uq.reshape(d, MLA_Q_RANK, MLA_HEADS, MLA_QK)
    zq = jnp.zeros((d, MLA_Q_RANK, MLA_HEADS, 16), w_mla_uq.dtype)
    wq = jnp.concatenate(
        [uq[..., 0:32], uq[..., 64:80], zq, uq[..., 32:64], uq[..., 80:96], zq],
        axis=-1).reshape(d, MLA_Q_RANK, MLA_HEADS * HEAD_PAD).astype(BF16)

    ukv = w_mla_ukv.reshape(d, MLA_KV_RANK, MLA_HEADS, MLA_NOPE + MLA_V)
    zk = jnp.zeros((d, MLA_KV_RANK, MLA_HEADS, 32), w_mla_ukv.dtype)
    wk = jnp.concatenate(
        [ukv[..., 0:32], zk, ukv[..., 32:64], zk],
        axis=-1).reshape(d, MLA_KV_RANK, MLA_HEADS * HEAD_PAD).astype(BF16)
    wv = ukv[..., MLA_NOPE:].reshape(d, MLA_KV_RANK, MLA_VAL_WIDTH).astype(BF16)
    return w_main, w_mid, w_gate, wa2, wq, wk, wv


def kernel(x, positions, norm_pre_mix, norm_post_mix, norm_pre_ffn, norm_post_ffn, w_in, w_pool,
           pool_scale, w_a, w_gla_a2, b_gla_a, gla_norm, w_b, mla_q_norm, w_mla_uq, mla_kv_norm,
           w_mla_ukv, w_c, w_o, w_ffn_gu, w_ffn_down):
    batch, seq, d_model = x.shape
    t = batch * seq
    x2 = x.reshape(t, d_model)
    c_tab, s_tab = _rope_tables(positions)
    w_main, w_mid, w_gate, wa2, wq, wk, wv = _prep_weights(w_in, w_gla_a2, w_mla_uq, w_mla_ukv)
    w_pool_b = w_pool.astype(BF16)
    w_a_b, w_b_b, w_c_b, w_o_b = (w.astype(BF16) for w in (w_a, w_b, w_c, w_o))
    w_gu_b = w_ffn_gu.astype(BF16)
    w_d_b = w_ffn_down.astype(BF16)
    vec = lambda a: a.reshape(DEPTH, 1, -1)
    n_pre_mix, n_post_mix, n_pre_ffn, n_post_ffn = (
        vec(a) for a in (norm_pre_mix, norm_post_mix, norm_pre_ffn, norm_post_ffn))
    pool_scale, b_gla_a, gla_norm, mla_q_norm, mla_kv_norm = (
        vec(a) for a in (pool_scale, b_gla_a, gla_norm, mla_q_norm, mla_kv_norm))
    for l in range(DEPTH):
        gq, gk, gv, gr, misc, glog, pa, mq, mk, mv = _front(
            x2, n_pre_mix, w_main, w_mid, w_gate, c_tab, s_tab, mla_q_norm, mla_kv_norm,
            wq, wk, wv, w_pool_b, pool_scale, l, seq)
        gb = _gla(gq, gk, gv, gr, misc, wa2, b_gla_a, gla_norm, l, batch, seq)
        mc = _attention(mq, mk, mv, batch, seq)
        x2 = _merge_ffn(x2, pa, gb, mc, glog, w_a_b, w_b_b, w_c_b, w_o_b, n_post_mix,
                        n_pre_ffn, w_gu_b, w_d_b, n_post_ffn, l)
    return x2.reshape(batch, seq, d_model)
```

```python
import functools

import jax
import jax.numpy as jnp
import numpy as np
from jax import lax
from jax.experimental import pallas as pl
from jax.experimental.pallas import tpu as pltpu

F32 = jnp.float32
BF16 = jnp.bfloat16

D_MODEL = 1024
DEPTH = 4
POOL_WIDTH = 512
POOL_WINDOWS = (2, 4, 8, 16)
POOL_GROUP_DIM = 128
GLA_HEADS = 4
GLA_DK = 64
GLA_DV = 128
GLA_KEY_WIDTH = GLA_HEADS * GLA_DK
GLA_VAL_WIDTH = GLA_HEADS * GLA_DV
GLA_GATE_RANK = 16
GLA_GATE_TAU = 16.0
GLA_CHUNK = 64
GLA_SUPER = 256
MLA_HEADS = 8
MLA_Q_RANK = 384
MLA_KV_RANK = 256
MLA_NOPE = 64
MLA_ROPE = 32
MLA_V = 64
MLA_QK = MLA_NOPE + MLA_ROPE
MLA_VAL_WIDTH = MLA_HEADS * MLA_V
ROPE_BASE = 10000.0
N_BRANCH = 3
D_FF = 2816
EPS = 1e-6
LOG2E = 1.4426950408889634

LANES = 128
HEAD_PAD = 128
VMEM_LIMIT = 56 * 1024 * 1024

IN_MAIN = 2048
IN_MID = MLA_Q_RANK + MLA_KV_RANK + LANES
IN_GATE = N_BRANCH * D_MODEL


def _dot(a, b):
    return jnp.dot(a, b, preferred_element_type=F32)


def _dot_nt(a, b):
    return lax.dot_general(a, b, (((1,), (1,)), ((), ())), preferred_element_type=F32)


def _dot_tn(a, b):
    return lax.dot_general(a, b, (((0,), (0,)), ((), ())), preferred_element_type=F32)


def _rms(x, g):
    return x * lax.rsqrt(jnp.mean(x * x, axis=-1, keepdims=True) + EPS) * g


def _sigmoid(x):
    return 0.5 * jnp.tanh(0.5 * x) + 0.5


def _params(n_axes):
    return pltpu.CompilerParams(
        dimension_semantics=("arbitrary",) * n_axes, vmem_limit_bytes=VMEM_LIMIT)


def _layer_spec(layer, tail, col=0):
    idx = (layer,) + (0,) * (len(tail) - 1) + (col,)
    return pl.BlockSpec((None,) + tuple(tail), lambda *_: idx, pipeline_mode=pl.Buffered(1))


ROPE_HALF = MLA_ROPE // 2
TOK_PER_ROW = LANES // ROPE_HALF


def _split3(x):
    hi = x.astype(BF16)
    r1 = x - hi.astype(F32)
    mid = r1.astype(BF16)
    lo = (r1 - mid.astype(F32)).astype(BF16)
    return hi, mid, lo


def _rope_kernel(pos_ref, invf_ref, ec_ref, es_ref, base_ref, c_ref, s_ref):
    ang = pos_ref[...].astype(F32) * invf_ref[...]
    cos3 = _split3(jnp.cos(ang))
    sin3 = _split3(jnp.sin(ang))
    for j in range(TOK_PER_ROW):
        ec, es = ec_ref[j], es_ref[j]
        c_ref[:, j, :] = sum(_dot(t, ec) for t in cos3) + base_ref[...]
        s_ref[:, j, :] = sum(_dot(t, es) for t in sin3)


def _rope_tables(positions):
    t = positions.size
    inv_freq = ROPE_BASE ** (-jnp.arange(0, MLA_ROPE, 2, dtype=F32) / MLA_ROPE)
    rows = t // TOK_PER_ROW
    pos_d = jnp.repeat(positions.reshape(t), ROPE_HALF).reshape(rows, LANES)
    invf_d = jnp.tile(inv_freq, TOK_PER_ROW).reshape(1, LANES)
    ec = np.zeros((TOK_PER_ROW, LANES, HEAD_PAD), np.float32)
    es = np.zeros((TOK_PER_ROW, LANES, HEAD_PAD), np.float32)
    for j in range(TOK_PER_ROW):
        for i in range(ROPE_HALF):
            ec[j, j * ROPE_HALF + i, 32 + i] = 1.0
            ec[j, j * ROPE_HALF + i, 96 + i] = 1.0
            es[j, j * ROPE_HALF + i, 32 + i] = -1.0
            es[j, j * ROPE_HALF + i, 96 + i] = 1.0
    base = np.zeros((1, HEAD_PAD), np.float32)
    base[0, 0:32] = 1.0
    base[0, 64:96] = 1.0
    c3, s3 = pl.pallas_call(
        _rope_kernel,
        out_shape=(jax.ShapeDtypeStruct((rows, TOK_PER_ROW, HEAD_PAD), F32),) * 2,
        compiler_params=pltpu.CompilerParams(vmem_limit_bytes=VMEM_LIMIT),
        name="rope_tables",
    )(pos_d, invf_d, jnp.asarray(ec, BF16), jnp.asarray(es, BF16), jnp.asarray(base))
    return c3.reshape(t, HEAD_PAD), s3.reshape(t, HEAD_PAD)


POOL_HALO = 16


def _front_kernel(x_ref, g_ref, wm_ref, wd_ref, wg_ref, c_ref, s_ref, qn_ref, kvn_ref,
                  wq_ref, wk_ref, wv_ref, wp_ref, ps_ref,
                  gq_ref, gk_ref, gv_ref, gr_ref, misc_ref, glog_ref, pa_ref, q_out, k_out, v_out,
                  halo_ref, *, tiles_per_seq):
    tm = x_ref.shape[0]
    tile_in_seq = pl.program_id(0) % tiles_per_seq

    @pl.when(tile_in_seq == 0)
    def _():
        halo_ref[...] = jnp.zeros_like(halo_ref)

    h = _rms(x_ref[...], g_ref[...]).astype(BF16)

    y_mid = _dot(h, wd_ref[...])
    cq = y_mid[:, :MLA_Q_RANK]
    ckv = y_mid[:, MLA_Q_RANK:MLA_Q_RANK + MLA_KV_RANK]
    misc = y_mid[:, MLA_Q_RANK + MLA_KV_RANK:]
    misc_ref[...] = misc.astype(BF16)

    y0 = _dot(h, wm_ref[:, :2 * POOL_WIDTH])
    u = y0[:, :POOL_WIDTH]
    gq_ref[...] = y0[:, POOL_WIDTH:POOL_WIDTH + GLA_KEY_WIDTH].astype(BF16)
    gk_ref[...] = y0[:, POOL_WIDTH + GLA_KEY_WIDTH:].astype(BF16)
    y1 = _dot(h, wm_ref[:, 2 * POOL_WIDTH:])
    gv_ref[...] = y1[:, :GLA_VAL_WIDTH].astype(BF16)
    gr_ref[...] = y1[:, GLA_VAL_WIDTH:].astype(BF16)

    cqn = _rms(cq, qn_ref[...]).astype(BF16)
    qf = _dot(cqn, wq_ref[...])
    ckvn = _rms(ckv, kvn_ref[...]).astype(BF16)
    kf = _dot(ckvn, wk_ref[...])
    v_out[...] = _dot(ckvn, wv_ref[...]).astype(BF16)

    ext = jnp.concatenate([halo_ref[...], u], axis=0)
    halo_ref[...] = u[tm - POOL_HALO:, :]
    pos = tile_in_seq * tm + lax.broadcasted_iota(jnp.int32, (tm, POOL_GROUP_DIM), 0)
    for g, w in enumerate(POOL_WINDOWS):
        cols = slice(g * POOL_GROUP_DIM, (g + 1) * POOL_GROUP_DIM)
        s = ext[:, cols]
        sh = 1
        while sh < w:
            s = s + pltpu.roll(s, sh, axis=0)
            sh *= 2
        cnt = jnp.minimum(pos + 1, w).astype(F32)
        diff = s[POOL_HALO:] / cnt - u[:, cols]
        pa_ref[:, cols] = (_dot(diff.astype(BF16), wp_ref[g]) * ps_ref[:, cols]).astype(BF16)

    for a in range(0, IN_GATE, 1024):
        glog_ref[:, a:a + 1024] = _dot(h, wg_ref[:, a:a + 1024]).astype(BF16)

    c_tab = c_ref[...]
    s_tab = s_ref[...]
    scale = MLA_QK ** -0.5 * LOG2E
    lane = lax.broadcasted_iota(jnp.int32, (1, HEAD_PAD), 1)
    rope_lane = ((lane >= 32) & (lane < 48)) | ((lane >= 96) & (lane < 112))
    kr = jnp.where(rope_lane, misc * c_tab + pltpu.roll(misc, 64, axis=1) * s_tab, 0.0)
    for hd in range(MLA_HEADS):
        cols = slice(hd * HEAD_PAD, (hd + 1) * HEAD_PAD)
        xq = qf[:, cols]
        q_out[:, cols] = ((xq * c_tab + pltpu.roll(xq, 64, axis=1) * s_tab) * scale).astype(BF16)
        k_out[:, cols] = (kf[:, cols] + kr).astype(BF16)


def _front(x2, g, w_main, w_mid, w_gate, c_tab, s_tab, qn, kvn, wq, wk, wv, wp, ps, layer, seq,
           tm=512):
    t = x2.shape[0]
    hw = MLA_HEADS * HEAD_PAD
    row = lambda w: pl.BlockSpec((tm, w), lambda i: (i, 0))
    outs = ((GLA_KEY_WIDTH, BF16), (GLA_KEY_WIDTH, BF16), (GLA_VAL_WIDTH, BF16),
            (GLA_VAL_WIDTH, BF16), (LANES, BF16), (IN_GATE, BF16), (POOL_WIDTH, BF16),
            (hw, BF16), (hw, BF16), (MLA_VAL_WIDTH, BF16))
    return pl.pallas_call(
        functools.partial(_front_kernel, tiles_per_seq=seq // tm),
        grid=(t // tm,),
        in_specs=[
            row(D_MODEL),
            _layer_spec(layer, (1, D_MODEL)),
            _layer_spec(layer, (D_MODEL, IN_MAIN)),
            _layer_spec(layer, (D_MODEL, IN_MID)),
            _layer_spec(layer, (D_MODEL, IN_GATE)),
            row(HEAD_PAD), row(HEAD_PAD),
            _layer_spec(layer, (1, MLA_Q_RANK)), _layer_spec(layer, (1, MLA_KV_RANK)),
            _layer_spec(layer, (MLA_Q_RANK, hw)), _layer_spec(layer, (MLA_KV_RANK, hw)),
            _layer_spec(layer, (MLA_KV_RANK, MLA_VAL_WIDTH)),
            _layer_spec(layer, wp.shape[1:]),
            _layer_spec(layer, (1, POOL_WIDTH)),
        ],
        out_specs=tuple(row(w) for w, _ in outs),
        out_shape=tuple(jax.ShapeDtypeStruct((t, w), dt) for w, dt in outs),
        scratch_shapes=[pltpu.VMEM((POOL_HALO, POOL_WIDTH), F32)],
        compiler_params=_params(1),
        name="front",
    )(x2, g, w_main, w_mid, w_gate, c_tab, s_tab, qn, kvn, wq, wk, wv, wp, ps)


def _gla_kernel(q_ref, k_ref, v_ref, r_ref, misc_ref, wa2_ref, ba_ref, gn_ref, o_ref):
    s_len = q_ref.shape[0]
    c = GLA_CHUNK
    sc = GLA_SUPER
    row = lax.broadcasted_iota(jnp.int32, (sc, sc), 0)
    col = lax.broadcasted_iota(jnp.int32, (sc, sc), 1)
    same_chunk = (row // c) == (col // c)
    tril = same_chunk & (row >= col)
    tri_bf = tril.astype(BF16)
    lane = lax.broadcasted_iota(jnp.int32, (1, 2 * GLA_DK), 1)
    head_lanes = [(lane >= hh * GLA_DK) & (lane < (hh + 1) * GLA_DK) for hh in range(2)]
    vcols = [slice(hh * GLA_DV, (hh + 1) * GLA_DV) for hh in range(2)]
    blk = lax.broadcasted_iota(jnp.int32, (sc, 2 * GLA_DK), 0) // c
    nsub = sc // c

    def spread(x):
        return jnp.concatenate(
            [jnp.where(blk == j, x, 0.0).astype(BF16) for j in range(nsub)], axis=1)

    blocks = [slice(n * sc, (n + 1) * sc) for n in range(s_len // sc)]

    z = _dot(misc_ref[...], wa2_ref[...]) + ba_ref[...]
    la = (jnp.minimum(z, 0.0) - jnp.log1p(jnp.exp(-jnp.abs(z)))) * (1.0 / GLA_GATE_TAU)
    la3 = jnp.concatenate(_split3(la), axis=1)
    cum3 = [_dot(tri_bf, la3[rows]) for rows in blocks]

    q_dec, k_inv, k_end, last = [], [], [], []
    for rows, c3 in zip(blocks, cum3):
        cum = c3[:, :LANES] + c3[:, LANES:2 * LANES] + c3[:, 2 * LANES:]
        ends = [cum[(j + 1) * c - 1:(j + 1) * c, :] for j in range(nsub)]
        tot = jnp.concatenate([jnp.broadcast_to(r, (c, 2 * GLA_DK)) for r in ends], axis=0)
        qf = q_ref[rows, :].astype(F32) * (GLA_DK ** -0.5)
        kf = k_ref[rows, :].astype(F32)
        q_dec.append(qf * jnp.exp(cum))
        k_inv.append((kf * jnp.exp(-cum)).astype(BF16))
        k_end.append(kf * jnp.exp(tot - cum))
        last.append(ends)

    upd = [sum(_dot_tn(v_ref[rows, vcols[hh]], spread(jnp.where(head_lanes[hh], ke, 0.0)))
               for hh in range(2)) for rows, ke in zip(blocks, k_end)]
    att = [[jnp.where(tril, _dot_nt(jnp.where(head_lanes[hh], qd, 0.0).astype(BF16), ki),
                      0.0).astype(BF16) for hh in range(2)] for qd, ki in zip(q_dec, k_inv)]

    st = jnp.zeros((GLA_DV, 2 * GLA_DK), F32)
    st_cat = []
    for ends, u in zip(last, upd):
        sts = []
        for j in range(nsub):
            sts.append(st.astype(BF16))
            st = st * jnp.exp(ends[j]) + u[:, j * LANES:(j + 1) * LANES]
        st_cat.append(jnp.concatenate(sts, axis=1))

    for rows, qd, a, sc_n in zip(blocks, q_dec, att, st_cat):
        for hh in range(2):
            o = _dot(a[hh], v_ref[rows, vcols[hh]])
            o = o + _dot_nt(spread(jnp.where(head_lanes[hh], qd, 0.0)), sc_n)
            o = _rms(o, gn_ref[:, vcols[hh]])
            rr = r_ref[rows, vcols[hh]].astype(F32)
            o_ref[rows, vcols[hh]] = (o * (rr * _sigmoid(rr))).astype(BF16)


def _gla(q, k, v, r, misc, wa2, ba, gn, layer, batch, seq):
    t = q.shape[0]
    kw = 2 * GLA_DK
    vw = 2 * GLA_DV
    return pl.pallas_call(
        _gla_kernel,
        grid=(batch, GLA_HEADS // 2),
        in_specs=[
            pl.BlockSpec((seq, kw), lambda b, p: (b, p)),
            pl.BlockSpec((seq, kw), lambda b, p: (b, p)),
            pl.BlockSpec((seq, vw), lambda b, p: (b, p)),
            pl.BlockSpec((seq, vw), lambda b, p: (b, p)),
            pl.BlockSpec((seq, LANES), lambda b, p: (b, 0)),
            pl.BlockSpec((None, LANES, kw), lambda b, p: (layer, 0, p)),
            pl.BlockSpec((None, 1, kw), lambda b, p: (layer, 0, p)),
            pl.BlockSpec((None, 1, vw), lambda b, p: (layer, 0, p)),
        ],
        out_specs=pl.BlockSpec((seq, vw), lambda b, p: (b, p)),
        out_shape=jax.ShapeDtypeStruct((t, GLA_VAL_WIDTH), BF16),
        compiler_params=_params(2),
        name="gla",
    )(q, k, v, r, misc, wa2, ba, gn)


ATT_TQ = 512
ATT_HALF = ATT_TQ // 2
ATT_AHEAD = 2


def _attn_scores(q_ref, k_ref, qi, cols):
    tq, hf = ATT_TQ, ATT_HALF
    r0 = qi * tq
    q = q_ref[r0:r0 + tq, cols]
    k_d = k_ref[r0:r0 + tq, cols]
    d0 = _dot_nt(k_d[:hf], q[:hf])
    d1 = _dot_nt(k_d, q[hf:])
    s_off = _dot_nt(k_ref[0:r0, cols], q) if qi > 0 else None
    return d0, d1, s_off


def _attn_finish(vt, qi, scores):
    tq, hf = ATT_TQ, ATT_HALF
    r0 = qi * tq
    neg = jnp.finfo(F32).min
    d0, d1, s_off = scores
    key0 = lax.broadcasted_iota(jnp.int32, (hf, hf), 0)
    qry0 = lax.broadcasted_iota(jnp.int32, (hf, hf), 1)
    key1 = lax.broadcasted_iota(jnp.int32, (tq, hf), 0)
    qry1 = lax.broadcasted_iota(jnp.int32, (tq, hf), 1)
    d0 = jnp.where(key0 <= qry0, d0, neg)
    d1 = jnp.where(key1 <= qry1 + hf, d1, neg)
    m0 = jnp.max(d0, axis=0, keepdims=True)
    m1 = jnp.max(d1, axis=0, keepdims=True)
    if s_off is not None:
        m0 = jnp.maximum(m0, jnp.max(s_off[:, :hf], axis=0, keepdims=True))
        m1 = jnp.maximum(m1, jnp.max(s_off[:, hf:], axis=0, keepdims=True))
        p_off0 = jnp.exp2(s_off[:, :hf] - m0)
        p_off1 = jnp.exp2(s_off[:, hf:] - m1)
        p_off = jnp.concatenate([p_off0.astype(BF16), p_off1.astype(BF16)], axis=1)
        o_off = _dot(vt[:, 0:r0], p_off)
    p0 = jnp.exp2(d0 - m0)
    p1 = jnp.exp2(d1 - m1)
    l0 = jnp.sum(p0, axis=0, keepdims=True)
    l1 = jnp.sum(p1, axis=0, keepdims=True)
    o0 = _dot(vt[:, r0:r0 + hf], p0.astype(BF16))
    o1 = _dot(vt[:, r0:r0 + tq], p1.astype(BF16))
    if s_off is not None:
        l0 = l0 + jnp.sum(p_off0, axis=0, keepdims=True)
        l1 = l1 + jnp.sum(p_off1, axis=0, keepdims=True)
        o0 = o0 + o_off[:, :hf]
        o1 = o1 + o_off[:, hf:]
    return o0 / l0, o1 / l1


def _attn_kernel(q_ref, k_ref, v_ref, o_ref):
    tq, hf = ATT_TQ, ATT_HALF
    head_cols = (slice(0, HEAD_PAD), slice(HEAD_PAD, 2 * HEAD_PAD))
    vt_pair = v_ref[...].astype(F32).T
    vrow = lax.broadcasted_iota(jnp.int32, vt_pair.shape, 0)
    vt_head = (jnp.where(vrow < MLA_V, vt_pair, 0.0).astype(BF16),
               jnp.where(vrow >= MLA_V, vt_pair, 0.0).astype(BF16))
    units = [(qi, hh) for qi in reversed(range(q_ref.shape[0] // tq)) for hh in range(2)]
    pending = [_attn_scores(q_ref, k_ref, u[0], head_cols[u[1]]) for u in units[:ATT_AHEAD]]
    prev = None
    for i, (qi, hh) in enumerate(units):
        if i + ATT_AHEAD < len(units):
            u = units[i + ATT_AHEAD]
            pending.append(_attn_scores(q_ref, k_ref, u[0], head_cols[u[1]]))
        cur = _attn_finish(vt_head[hh], qi, pending.pop(0))
        if hh == 0:
            prev = cur
        else:
            o_ref[qi * tq:qi * tq + hf, :] = (prev[0] + cur[0]).T.astype(BF16)
            o_ref[qi * tq + hf:(qi + 1) * tq, :] = (prev[1] + cur[1]).T.astype(BF16)


def _attention(q, k, v, batch, seq):
    t = q.shape[0]
    pw = 2 * HEAD_PAD
    pair = pl.BlockSpec((seq, pw), lambda b, p: (b, p))
    return pl.pallas_call(
        _attn_kernel,
        grid=(batch, MLA_HEADS // 2),
        in_specs=[pair, pair, pl.BlockSpec((seq, HEAD_PAD), lambda b, p: (b, p))],
        out_specs=pl.BlockSpec((seq, HEAD_PAD), lambda b, p: (b, p)),
        out_shape=jax.ShapeDtypeStruct((t, MLA_VAL_WIDTH), BF16),
        compiler_params=_params(2),
        name="attention",
    )(q, k, v)


FFN_CHUNKS = ((0, 1024), (1024, 2048), (2048, D_FF))
MERGE_FFN_VMEM = 60 * 1024 * 1024
MERGE_FFN_SUB = 2


def _merge_ffn_kernel(x_ref, pa_ref, gb_ref, mc_ref, gl_ref, wa_ref, wb_ref, wc_ref, wo_ref,
                      n_mix_ref, n1_ref, wg_ref, wu_ref, wd_ref, n2_ref, o_ref):
    tm = x_ref.shape[0]
    sub = tm // MERGE_FFN_SUB
    groups = [slice(s * sub, (s + 1) * sub) for s in range(MERGE_FFN_SUB)]
    branches = ((pa_ref, wa_ref), (gb_ref, wb_ref), (mc_ref, wc_ref))

    merged = []
    for rows in groups:
        m = None
        for i, (b_ref, w_ref) in enumerate(branches):
            y = _dot(b_ref[rows, :], w_ref[...])
            gate = _sigmoid(gl_ref[rows, i * D_MODEL:(i + 1) * D_MODEL].astype(F32))
            m = gate * y if m is None else m + gate * y
        merged.append(m.astype(BF16))
    xs, hs = [], []
    for rows, m in zip(groups, merged):
        x = x_ref[rows, :] + _rms(_dot(m, wo_ref[...]), n_mix_ref[...])
        xs.append(x)
        hs.append(_rms(x, n1_ref[...]).astype(BF16))
    accs = [None] * len(groups)
    for a, b in FFN_CHUNKS:
        gu = [(_dot(h, wg_ref[:, a:b]), _dot(h, wu_ref[:, a:b])) for h in hs]
        for s, (g, u) in enumerate(gu):
            d = _dot((g * _sigmoid(g) * u).astype(BF16), wd_ref[a:b, :])
            accs[s] = d if accs[s] is None else accs[s] + d
    for rows, x, acc in zip(groups, xs, accs):
        o_ref[rows, :] = x + _rms(acc, n2_ref[...])


def _merge_ffn(x2, pa, gb, mc, glog, wa, wb, wc, wo, n_mix, n1, wgu, wd, n2, layer, tm=512):
    t = x2.shape[0]
    row = lambda w: pl.BlockSpec((tm, w), lambda i: (i, 0))
    return pl.pallas_call(
        _merge_ffn_kernel,
        grid=(t // tm,),
        in_specs=[
            row(D_MODEL), row(POOL_WIDTH), row(GLA_VAL_WIDTH), row(MLA_VAL_WIDTH),
            row(N_BRANCH * D_MODEL),
            _layer_spec(layer, (POOL_WIDTH, D_MODEL)), _layer_spec(layer, (GLA_VAL_WIDTH, D_MODEL)),
            _layer_spec(layer, (MLA_VAL_WIDTH, D_MODEL)), _layer_spec(layer, (D_MODEL, D_MODEL)),
            _layer_spec(layer, (1, D_MODEL)), _layer_spec(layer, (1, D_MODEL)),
            _layer_spec(layer, (D_MODEL, D_FF), col=0), _layer_spec(layer, (D_MODEL, D_FF), col=1),
            _layer_spec(layer, (D_FF, D_MODEL)), _layer_spec(layer, (1, D_MODEL)),
        ],
        out_specs=row(D_MODEL),
        out_shape=jax.ShapeDtypeStruct((t, D_MODEL), F32),
        compiler_params=pltpu.CompilerParams(
            dimension_semantics=("arbitrary",), vmem_limit_bytes=MERGE_FFN_VMEM),
        name="merge_ffn",
    )(x2, pa, gb, mc, glog, wa, wb, wc, wo, n_mix, n1, wgu, wgu, wd, n2)


W_SPLIT_ROWS = 256
W_IN_GATE_OFF = IN_MAIN + GLA_GATE_RANK + MLA_Q_RANK + MLA_KV_RANK + MLA_ROPE


def _w_in_split_kernel(w_ref, main_ref, gate_ref):
    main_ref[...] = w_ref[:, :IN_MAIN].astype(BF16)
    gate_ref[...] = w_ref[:, W_IN_GATE_OFF:W_IN_GATE_OFF + IN_GATE].astype(BF16)


def _w_in_split(w_in):
    d, k, n = w_in.shape
    return pl.pallas_call(
        _w_in_split_kernel,
        grid=(d, k // W_SPLIT_ROWS),
        in_specs=[pl.BlockSpec((None, W_SPLIT_ROWS, n), lambda l, r: (l, r, 0))],
        out_specs=(pl.BlockSpec((None, W_SPLIT_ROWS, IN_MAIN), lambda l, r: (l, r, 0)),
                   pl.BlockSpec((None, W_SPLIT_ROWS, IN_GATE), lambda l, r: (l, r, 0))),
        out_shape=(jax.ShapeDtypeStruct((d, k, IN_MAIN), BF16),
                   jax.ShapeDtypeStruct((d, k, IN_GATE), BF16)),
        compiler_params=_params(2),
        name="w_in_split",
    )(w_in)


def _prep_weights(w_in, w_gla_a2, w_mla_uq, w_mla_ukv):
    sizes = (POOL_WIDTH, GLA_KEY_WIDTH, GLA_KEY_WIDTH, GLA_VAL_WIDTH, GLA_VAL_WIDTH,
             GLA_GATE_RANK, MLA_Q_RANK, MLA_KV_RANK, MLA_ROPE, N_BRANCH * D_MODEL)
    offs = [0]
    for s in sizes:
        offs.append(offs[-1] + s)
    assert offs[5] == IN_MAIN and offs[9] == W_IN_GATE_OFF and offs[10] == w_in.shape[-1]
    w_main, w_gate = _w_in_split(w_in)
    w_a1, w_cq, w_ckv, w_kr = (w_in[:, :, offs[i]:offs[i + 1]] for i in range(5, 9))
    d = w_in.shape[0]
    z16 = jnp.zeros((d, D_MODEL, 16), w_in.dtype)
    z48 = jnp.zeros((d, D_MODEL, 48), w_in.dtype)
    w_mid = jnp.concatenate(
        [w_cq, w_ckv, w_a1, z16, w_kr[..., :16], z48, w_kr[..., 16:], z16], axis=-1).astype(BF16)

    wa2 = jnp.pad(w_gla_a2, ((0, 0), (0, LANES - GLA_GATE_RANK), (0, 0))).astype(BF16)

    uq = w_mla_uq.reshape(d, MLA_Q_RANK, MLA_HEADS, MLA_QK)
    zq = jnp.zeros((d, MLA_Q_RANK, MLA_HEADS, 16), w_mla_uq.dtype)
    wq = jnp.concatenate(
        [uq[..., 0:32], uq[..., 64:80], zq, uq[..., 32:64], uq[..., 80:96], zq],
        axis=-1).reshape(d, MLA_Q_RANK, MLA_HEADS * HEAD_PAD).astype(BF16)

    ukv = w_mla_ukv.reshape(d, MLA_KV_RANK, MLA_HEADS, MLA_NOPE + MLA_V)
    zk = jnp.zeros((d, MLA_KV_RANK, MLA_HEADS, 32), w_mla_ukv.dtype)
    wk = jnp.concatenate(
        [ukv[..., 0:32], zk, ukv[..., 32:64], zk],
        axis=-1).reshape(d, MLA_KV_RANK, MLA_HEADS * HEAD_PAD).astype(BF16)
    wv = ukv[..., MLA_NOPE:].reshape(d, MLA_KV_RANK, MLA_VAL_WIDTH).astype(BF16)
    return w_main, w_mid, w_gate, wa2, wq, wk, wv


def kernel(x, positions, norm_pre_mix, norm_post_mix, norm_pre_ffn, norm_post_ffn, w_in, w_pool,
           pool_scale, w_a, w_gla_a2, b_gla_a, gla_norm, w_b, mla_q_norm, w_mla_uq, mla_kv_norm,
           w_mla_ukv, w_c, w_o, w_ffn_gu, w_ffn_down):
    batch, seq, d_model = x.shape
    t = batch * seq
    x2 = x.reshape(t, d_model)
    c_tab, s_tab = _rope_tables(positions)
    w_main, w_mid, w_gate, wa2, wq, wk, wv = _prep_weights(w_in, w_gla_a2, w_mla_uq, w_mla_ukv)
    w_pool_b = w_pool.astype(BF16)
    w_a_b, w_b_b, w_c_b, w_o_b = (w.astype(BF16) for w in (w_a, w_b, w_c, w_o))
    w_gu_b = w_ffn_gu.astype(BF16)
    w_d_b = w_ffn_down.astype(BF16)
    vec = lambda a: a.reshape(DEPTH, 1, -1)
    n_pre_mix, n_post_mix, n_pre_ffn, n_post_ffn = (
        vec(a) for a in (norm_pre_mix, norm_post_mix, norm_pre_ffn, norm_post_ffn))
    pool_scale, b_gla_a, gla_norm, mla_q_norm, mla_kv_norm = (
        vec(a) for a in (pool_scale, b_gla_a, gla_norm, mla_q_norm, mla_kv_norm))
    for l in range(DEPTH):
        gq, gk, gv, gr, misc, glog, pa, mq, mk, mv = _front(
            x2, n_pre_mix, w_main, w_mid, w_gate, c_tab, s_tab, mla_q_norm, mla_kv_norm,
            wq, wk, wv, w_pool_b, pool_scale, l, seq)
        gb = _gla(gq, gk, gv, gr, misc, wa2, b_gla_a, gla_norm, l, batch, seq)
        mc = _attention(mq, mk, mv, batch, seq)
        x2 = _merge_ffn(x2, pa, gb, mc, glog, w_a_b, w_b_b, w_c_b, w_o_b, n_post_mix,
                        n_pre_ffn, w_gu_b, w_d_b, n_post_ffn, l)
    return x2.reshape(batch, seq, d_model)
```

```python
import functools

import jax
import jax.numpy as jnp
import numpy as np
from jax import lax
from jax.experimental import pallas as pl
from jax.experimental.pallas import tpu as pltpu

F32 = jnp.float32
BF16 = jnp.bfloat16

D_MODEL = 1024
DEPTH = 4
POOL_WIDTH = 512
POOL_WINDOWS = (2, 4, 8, 16)
POOL_GROUP_DIM = 128
GLA_HEADS = 4
GLA_DK = 64
GLA_DV = 128
GLA_KEY_WIDTH = GLA_HEADS * GLA_DK
GLA_VAL_WIDTH = GLA_HEADS * GLA_DV
GLA_GATE_RANK = 16
GLA_GATE_TAU = 16.0
GLA_CHUNK = 64
GLA_SUPER = 256
MLA_HEADS = 8
MLA_Q_RANK = 384
MLA_KV_RANK = 256
MLA_NOPE = 64
MLA_ROPE = 32
MLA_V = 64
MLA_QK = MLA_NOPE + MLA_ROPE
MLA_VAL_WIDTH = MLA_HEADS * MLA_V
ROPE_BASE = 10000.0
N_BRANCH = 3
D_FF = 2816
EPS = 1e-6
LOG2E = 1.4426950408889634

LANES = 128
HEAD_PAD = 128
VMEM_LIMIT = 56 * 1024 * 1024

IN_MAIN = 2048
IN_MID = MLA_Q_RANK + MLA_KV_RANK + LANES
IN_GATE = N_BRANCH * D_MODEL


def _dot(a, b):
    return jnp.dot(a, b, preferred_element_type=F32)


def _dot_nt(a, b):
    return lax.dot_general(a, b, (((1,), (1,)), ((), ())), preferred_element_type=F32)


def _dot_tn(a, b):
    return lax.dot_general(a, b, (((0,), (0,)), ((), ())), preferred_element_type=F32)


def _rms(x, g):
    return x * lax.rsqrt(jnp.mean(x * x, axis=-1, keepdims=True) + EPS) * g


def _sigmoid(x):
    return 0.5 * jnp.tanh(0.5 * x) + 0.5


def _params(n_axes):
    return pltpu.CompilerParams(
        dimension_semantics=("arbitrary",) * n_axes, vmem_limit_bytes=VMEM_LIMIT)


def _layer_spec(layer, tail, col=0):
    idx = (layer,) + (0,) * (len(tail) - 1) + (col,)
    return pl.BlockSpec((None,) + tuple(tail), lambda *_: idx, pipeline_mode=pl.Buffered(1))


ROPE_HALF = MLA_ROPE // 2
TOK_PER_ROW = LANES // ROPE_HALF


def _split3(x):
    hi = x.astype(BF16)
    r1 = x - hi.astype(F32)
    mid = r1.astype(BF16)
    lo = (r1 - mid.astype(F32)).astype(BF16)
    return hi, mid, lo


def _rope_kernel(pos_ref, invf_ref, ec_ref, es_ref, base_ref, c_ref, s_ref):
    ang = pos_ref[...].astype(F32) * invf_ref[...]
    cos3 = _split3(jnp.cos(ang))
    sin3 = _split3(jnp.sin(ang))
    for j in range(TOK_PER_ROW):
        ec, es = ec_ref[j], es_ref[j]
        c_ref[:, j, :] = sum(_dot(t, ec) for t in cos3) + base_ref[...]
        s_ref[:, j, :] = sum(_dot(t, es) for t in sin3)


def _rope_tables(positions):
    t = positions.size
    inv_freq = ROPE_BASE ** (-jnp.arange(0, MLA_ROPE, 2, dtype=F32) / MLA_ROPE)
    rows = t // TOK_PER_ROW
    pos_d = jnp.repeat(positions.reshape(t), ROPE_HALF).reshape(rows, LANES)
    invf_d = jnp.tile(inv_freq, TOK_PER_ROW).reshape(1, LANES)
    ec = np.zeros((TOK_PER_ROW, LANES, HEAD_PAD), np.float32)
    es = np.zeros((TOK_PER_ROW, LANES, HEAD_PAD), np.float32)
    for j in range(TOK_PER_ROW):
        for i in range(ROPE_HALF):
            ec[j, j * ROPE_HALF + i, 32 + i] = 1.0
            ec[j, j * ROPE_HALF + i, 96 + i] = 1.0
            es[j, j * ROPE_HALF + i, 32 + i] = -1.0
            es[j, j * ROPE_HALF + i, 96 + i] = 1.0
    base = np.zeros((1, HEAD_PAD), np.float32)
    base[0, 0:32] = 1.0
    base[0, 64:96] = 1.0
    c3, s3 = pl.pallas_call(
        _rope_kernel,
        out_shape=(jax.ShapeDtypeStruct((rows, TOK_PER_ROW, HEAD_PAD), F32),) * 2,
        compiler_params=pltpu.CompilerParams(vmem_limit_bytes=VMEM_LIMIT),
        name="rope_tables",
    )(pos_d, invf_d, jnp.asarray(ec, BF16), jnp.asarray(es, BF16), jnp.asarray(base))
    return c3.reshape(t, HEAD_PAD), s3.reshape(t, HEAD_PAD)


POOL_HALO = 16


def _front_kernel(x_ref, g_ref, wm_ref, wd_ref, wg_ref, c_ref, s_ref, qn_ref, kvn_ref,
                  wq_ref, wk_ref, wv_ref, wp_ref, ps_ref,
                  gq_ref, gk_ref, gv_ref, gr_ref, misc_ref, glog_ref, pa_ref, q_out, k_out, v_out,
                  halo_ref, *, tiles_per_seq):
    tm = x_ref.shape[0]
    tile_in_seq = pl.program_id(0) % tiles_per_seq

    @pl.when(tile_in_seq == 0)
    def _():
        halo_ref[...] = jnp.zeros_like(halo_ref)

    h = _rms(x_ref[...], g_ref[...]).astype(BF16)

    y_mid = _dot(h, wd_ref[...])
    cq = y_mid[:, :MLA_Q_RANK]
    ckv = y_mid[:, MLA_Q_RANK:MLA_Q_RANK + MLA_KV_RANK]
    misc = y_mid[:, MLA_Q_RANK + MLA_KV_RANK:]
    misc_ref[...] = misc.astype(BF16)

    y0 = _dot(h, wm_ref[:, :2 * POOL_WIDTH])
    u = y0[:, :POOL_WIDTH]
    gq_ref[...] = y0[:, POOL_WIDTH:POOL_WIDTH + GLA_KEY_WIDTH].astype(BF16)
    gk_ref[...] = y0[:, POOL_WIDTH + GLA_KEY_WIDTH:].astype(BF16)
    y1 = _dot(h, wm_ref[:, 2 * POOL_WIDTH:])
    gv_ref[...] = y1[:, :GLA_VAL_WIDTH].astype(BF16)
    gr_ref[...] = y1[:, GLA_VAL_WIDTH:].astype(BF16)

    cqn = _rms(cq, qn_ref[...]).astype(BF16)
    qf = _dot(cqn, wq_ref[...])
    ckvn = _rms(ckv, kvn_ref[...]).astype(BF16)
    kf = _dot(ckvn, wk_ref[...])
    v_out[...] = _dot(ckvn, wv_ref[...]).astype(BF16)

    ext = jnp.concatenate([halo_ref[...], u], axis=0)
    halo_ref[...] = u[tm - POOL_HALO:, :]
    pos = tile_in_seq * tm + lax.broadcasted_iota(jnp.int32, (tm, POOL_GROUP_DIM), 0)
    for g, w in enumerate(POOL_WINDOWS):
        cols = slice(g * POOL_GROUP_DIM, (g + 1) * POOL_GROUP_DIM)
        s = ext[:, cols]
        sh = 1
        while sh < w:
            s = s + pltpu.roll(s, sh, axis=0)
            sh *= 2
        cnt = jnp.minimum(pos + 1, w).astype(F32)
        diff = s[POOL_HALO:] / cnt - u[:, cols]
        pa_ref[:, cols] = (_dot(diff.astype(BF16), wp_ref[g]) * ps_ref[:, cols]).astype(BF16)

    for a in range(0, IN_GATE, 1024):
        glog_ref[:, a:a + 1024] = _dot(h, wg_ref[:, a:a + 1024]).astype(BF16)

    c_tab = c_ref[...]
    s_tab = s_ref[...]
    scale = MLA_QK ** -0.5 * LOG2E
    lane = lax.broadcasted_iota(jnp.int32, (1, HEAD_PAD), 1)
    rope_lane = ((lane >= 32) & (lane < 48)) | ((lane >= 96) & (lane < 112))
    kr = jnp.where(rope_lane, misc * c_tab + pltpu.roll(misc, 64, axis=1) * s_tab, 0.0)
    for hd in range(MLA_HEADS):
        cols = slice(hd * HEAD_PAD, (hd + 1) * HEAD_PAD)
        xq = qf[:, cols]
        q_out[:, cols] = ((xq * c_tab + pltpu.roll(xq, 64, axis=1) * s_tab) * scale).astype(BF16)
        k_out[:, cols] = (kf[:, cols] + kr).astype(BF16)


def _front(x2, g, w_main, w_mid, w_gate, c_tab, s_tab, qn, kvn, wq, wk, wv, wp, ps, layer, seq,
           tm=512):
    t = x2.shape[0]
    hw = MLA_HEADS * HEAD_PAD
    row = lambda w: pl.BlockSpec((tm, w), lambda i: (i, 0))
    outs = ((GLA_KEY_WIDTH, BF16), (GLA_KEY_WIDTH, BF16), (GLA_VAL_WIDTH, BF16),
            (GLA_VAL_WIDTH, BF16), (LANES, BF16), (IN_GATE, BF16), (POOL_WIDTH, BF16),
            (hw, BF16), (hw, BF16), (MLA_VAL_WIDTH, BF16))
    return pl.pallas_call(
        functools.partial(_front_kernel, tiles_per_seq=seq // tm),
        grid=(t // tm,),
        in_specs=[
            row(D_MODEL),
            _layer_spec(layer, (1, D_MODEL)),
            _layer_spec(layer, (D_MODEL, IN_MAIN)),
            _layer_spec(layer, (D_MODEL, IN_MID)),
            _layer_spec(layer, (D_MODEL, IN_GATE)),
            row(HEAD_PAD), row(HEAD_PAD),
            _layer_spec(layer, (1, MLA_Q_RANK)), _layer_spec(layer, (1, MLA_KV_RANK)),
            _layer_spec(layer, (MLA_Q_RANK, hw)), _layer_spec(layer, (MLA_KV_RANK, hw)),
            _layer_spec(layer, (MLA_KV_RANK, MLA_VAL_WIDTH)),
            _layer_spec(layer, wp.shape[1:]),
            _layer_spec(layer, (1, POOL_WIDTH)),
        ],
        out_specs=tuple(row(w) for w, _ in outs),
        out_shape=tuple(jax.ShapeDtypeStruct((t, w), dt) for w, dt in outs),
        scratch_shapes=[pltpu.VMEM((POOL_HALO, POOL_WIDTH), F32)],
        compiler_params=_params(1),
        name="front",
    )(x2, g, w_main, w_mid, w_gate, c_tab, s_tab, qn, kvn, wq, wk, wv, wp, ps)


def _gla_kernel(q_ref, k_ref, v_ref, r_ref, misc_ref, wa2_ref, ba_ref, gn_ref, o_ref):
    s_len = q_ref.shape[0]
    c = GLA_CHUNK
    sc = GLA_SUPER
    row = lax.broadcasted_iota(jnp.int32, (sc, sc), 0)
    col = lax.broadcasted_iota(jnp.int32, (sc, sc), 1)
    same_chunk = (row // c) == (col // c)
    tril = same_chunk & (row >= col)
    tri_bf = tril.astype(BF16)
    lane = lax.broadcasted_iota(jnp.int32, (1, 2 * GLA_DK), 1)
    head_lanes = [(lane >= hh * GLA_DK) & (lane < (hh + 1) * GLA_DK) for hh in range(2)]
    vcols = [slice(hh * GLA_DV, (hh + 1) * GLA_DV) for hh in range(2)]
    blk = lax.broadcasted_iota(jnp.int32, (sc, 2 * GLA_DK), 0) // c
    nsub = sc // c

    def spread(x):
        return jnp.concatenate(
            [jnp.where(blk == j, x, 0.0).astype(BF16) for j in range(nsub)], axis=1)

    blocks = [slice(n * sc, (n + 1) * sc) for n in range(s_len // sc)]

    z = _dot(misc_ref[...], wa2_ref[...]) + ba_ref[...]
    la = (jnp.minimum(z, 0.0) - jnp.log1p(jnp.exp(-jnp.abs(z)))) * (1.0 / GLA_GATE_TAU)
    la3 = jnp.concatenate(_split3(la), axis=1)
    cum3 = [_dot(tri_bf, la3[rows]) for rows in blocks]

    q_dec, k_inv, k_end, last = [], [], [], []
    for rows, c3 in zip(blocks, cum3):
        cum = c3[:, :LANES] + c3[:, LANES:2 * LANES] + c3[:, 2 * LANES:]
        ends = [cum[(j + 1) * c - 1:(j + 1) * c, :] for j in range(nsub)]
        tot = jnp.concatenate([jnp.broadcast_to(r, (c, 2 * GLA_DK)) for r in ends], axis=0)
        qf = q_ref[rows, :].astype(F32) * (GLA_DK ** -0.5)
        kf = k_ref[rows, :].astype(F32)
        q_dec.append(qf * jnp.exp(cum))
        k_inv.append((kf * jnp.exp(-cum)).astype(BF16))
        k_end.append(kf * jnp.exp(tot - cum))
        last.append(ends)

    upd = [sum(_dot_tn(v_ref[rows, vcols[hh]], spread(jnp.where(head_lanes[hh], ke, 0.0)))
               for hh in range(2)) for rows, ke in zip(blocks, k_end)]
    att = [[jnp.where(tril, _dot_nt(jnp.where(head_lanes[hh], qd, 0.0).astype(BF16), ki),
                      0.0).astype(BF16) for hh in range(2)] for qd, ki in zip(q_dec, k_inv)]

    st = jnp.zeros((GLA_DV, 2 * GLA_DK), F32)
    st_cat = []
    for ends, u in zip(last, upd):
        sts = []
        for j in range(nsub):
            sts.append(st.astype(BF16))
            st = st * jnp.exp(ends[j]) + u[:, j * LANES:(j + 1) * LANES]
        st_cat.append(jnp.concatenate(sts, axis=1))

    for rows, qd, a, sc_n in zip(blocks, q_dec, att, st_cat):
        for hh in range(2):
            o = _dot(a[hh], v_ref[rows, vcols[hh]])
            o = o + _dot_nt(spread(jnp.where(head_lanes[hh], qd, 0.0)), sc_n)
            o = _rms(o, gn_ref[:, vcols[hh]])
            rr = r_ref[rows, vcols[hh]].astype(F32)
            o_ref[rows, vcols[hh]] = (o * (rr * _sigmoid(rr))).astype(BF16)


def _gla(q, k, v, r, misc, wa2, ba, gn, layer, batch, seq):
    t = q.shape[0]
    kw = 2 * GLA_DK
    vw = 2 * GLA_DV
    return pl.pallas_call(
        _gla_kernel,
        grid=(batch, GLA_HEADS // 2),
        in_specs=[
            pl.BlockSpec((seq, kw), lambda b, p: (b, p)),
            pl.BlockSpec((seq, kw), lambda b, p: (b, p)),
            pl.BlockSpec((seq, vw), lambda b, p: (b, p)),
            pl.BlockSpec((seq, vw), lambda b, p: (b, p)),
            pl.BlockSpec((seq, LANES), lambda b, p: (b, 0)),
            pl.BlockSpec((None, LANES, kw), lambda b, p: (layer, 0, p)),
            pl.BlockSpec((None, 1, kw), lambda b, p: (layer, 0, p)),
            pl.BlockSpec((None, 1, vw), lambda b, p: (layer, 0, p)),
        ],
        out_specs=pl.BlockSpec((seq, vw), lambda b, p: (b, p)),
        out_shape=jax.ShapeDtypeStruct((t, GLA_VAL_WIDTH), BF16),
        compiler_params=_params(2),
        name="gla",
    )(q, k, v, r, misc, wa2, ba, gn)


ATT_TQ = 512
ATT_HALF = ATT_TQ // 2
ATT_AHEAD = 2


def _attn_scores(q_ref, k_ref, qi, cols):
    tq, hf = ATT_TQ, ATT_HALF
    r0 = qi * tq
    q = q_ref[r0:r0 + tq, cols]
    k_d = k_ref[r0:r0 + tq, cols]
    d0 = _dot_nt(k_d[:hf], q[:hf])
    d1 = _dot_nt(k_d, q[hf:])
    s_off = _dot_nt(k_ref[0:r0, cols], q) if qi > 0 else None
    return d0, d1, s_off


def _attn_causal_masks():
    tq, hf = ATT_TQ, ATT_HALF
    key0 = lax.broadcasted_iota(jnp.int32, (hf, hf), 0)
    qry0 = lax.broadcasted_iota(jnp.int32, (hf, hf), 1)
    key1 = lax.broadcasted_iota(jnp.int32, (tq, hf), 0)
    qry1 = lax.broadcasted_iota(jnp.int32, (tq, hf), 1)
    return key0 <= qry0, key1 <= qry1 + hf


def _attn_finish(vt, ones_row, masks, qi, scores):
    tq, hf = ATT_TQ, ATT_HALF
    r0 = qi * tq
    neg = jnp.finfo(F32).min
    d0, d1, s_off = scores
    d0 = jnp.where(masks[0], d0, neg)
    d1 = jnp.where(masks[1], d1, neg)
    m0 = jnp.max(d0, axis=0, keepdims=True)
    m1 = jnp.max(d1, axis=0, keepdims=True)
    if s_off is not None:
        m0 = jnp.maximum(m0, jnp.max(s_off[:, :hf], axis=0, keepdims=True))
        m1 = jnp.maximum(m1, jnp.max(s_off[:, hf:], axis=0, keepdims=True))
        p_off0 = jnp.exp2(s_off[:, :hf] - m0)
        p_off1 = jnp.exp2(s_off[:, hf:] - m1)
        p_off = jnp.concatenate([p_off0.astype(BF16), p_off1.astype(BF16)], axis=1)
        o_off = _dot(vt[:, 0:r0], p_off)
    p0 = jnp.exp2(d0 - m0)
    p1 = jnp.exp2(d1 - m1)
    o0 = _dot(vt[:, r0:r0 + hf], p0.astype(BF16))
    o1 = _dot(vt[:, r0:r0 + tq], p1.astype(BF16))
    if s_off is not None:
        o0 = o0 + o_off[:, :hf]
        o1 = o1 + o_off[:, hf:]
    is_ones = lax.broadcasted_iota(jnp.int32, (HEAD_PAD, hf), 0) == ones_row
    return tuple(jnp.where(is_ones, 0.0, o / o[ones_row:ones_row + 1, :]) for o in (o0, o1))


def _attn_kernel(q_ref, k_ref, v_ref, o_ref):
    tq, hf = ATT_TQ, ATT_HALF
    head_cols = (slice(0, HEAD_PAD), slice(HEAD_PAD, 2 * HEAD_PAD))
    vt_pair = v_ref[...].astype(F32).T
    vrow = lax.broadcasted_iota(jnp.int32, vt_pair.shape, 0)
    masks = _attn_causal_masks()
    ones_rows = (MLA_V, 0)
    vt_head = (
        jnp.where(vrow < MLA_V, vt_pair, jnp.where(vrow == ones_rows[0], 1.0, 0.0)).astype(BF16),
        jnp.where(vrow >= MLA_V, vt_pair, jnp.where(vrow == ones_rows[1], 1.0, 0.0)).astype(BF16))
    units = [(qi, hh) for qi in reversed(range(q_ref.shape[0] // tq)) for hh in range(2)]
    pending = [_attn_scores(q_ref, k_ref, u[0], head_cols[u[1]]) for u in units[:ATT_AHEAD]]
    prev = None
    for i, (qi, hh) in enumerate(units):
        if i + ATT_AHEAD < len(units):
            u = units[i + ATT_AHEAD]
            pending.append(_attn_scores(q_ref, k_ref, u[0], head_cols[u[1]]))
        cur = _attn_finish(vt_head[hh], ones_rows[hh], masks, qi, pending.pop(0))
        if hh == 0:
            prev = cur
        else:
            o_ref[qi * tq:qi * tq + hf, :] = (prev[0] + cur[0]).T.astype(BF16)
            o_ref[qi * tq + hf:(qi + 1) * tq, :] = (prev[1] + cur[1]).T.astype(BF16)


def _attention(q, k, v, batch, seq):
    t = q.shape[0]
    pw = 2 * HEAD_PAD
    pair = pl.BlockSpec((seq, pw), lambda b, p: (b, p))
    return pl.pallas_call(
        _attn_kernel,
        grid=(batch, MLA_HEADS // 2),
        in_specs=[pair, pair, pl.BlockSpec((seq, HEAD_PAD), lambda b, p: (b, p))],
        out_specs=pl.BlockSpec((seq, HEAD_PAD), lambda b, p: (b, p)),
        out_shape=jax.ShapeDtypeStruct((t, MLA_VAL_WIDTH), BF16),
        compiler_params=_params(2),
        name="attention",
    )(q, k, v)


FFN_CHUNKS = ((0, 1024), (1024, 2048), (2048, D_FF))
MERGE_FFN_VMEM = 60 * 1024 * 1024
MERGE_FFN_SUB = 2


def _merge_ffn_kernel(x_ref, pa_ref, gb_ref, mc_ref, gl_ref, wa_ref, wb_ref, wc_ref, wo_ref,
                      n_mix_ref, n1_ref, wg_ref, wu_ref, wd_ref, n2_ref, o_ref):
    tm = x_ref.shape[0]
    sub = tm // MERGE_FFN_SUB
    groups = [slice(s * sub, (s + 1) * sub) for s in range(MERGE_FFN_SUB)]
    branches = ((pa_ref, wa_ref), (gb_ref, wb_ref), (mc_ref, wc_ref))

    merged = []
    for rows in groups:
        m = None
        for i, (b_ref, w_ref) in enumerate(branches):
            y = _dot(b_ref[rows, :], w_ref[...])
            gate = _sigmoid(gl_ref[rows, i * D_MODEL:(i + 1) * D_MODEL].astype(F32))
            m = gate * y if m is None else m + gate * y
        merged.append(m.astype(BF16))
    xs, hs = [], []
    for rows, m in zip(groups, merged):
        x = x_ref[rows, :] + _rms(_dot(m, wo_ref[...]), n_mix_ref[...])
        xs.append(x)
        hs.append(_rms(x, n1_ref[...]).astype(BF16))
    accs = [None] * len(groups)
    for a, b in FFN_CHUNKS:
        gu = [(_dot(h, wg_ref[:, a:b]), _dot(h, wu_ref[:, a:b])) for h in hs]
        for s, (g, u) in enumerate(gu):
            d = _dot((g * _sigmoid(g) * u).astype(BF16), wd_ref[a:b, :])
            accs[s] = d if accs[s] is None else accs[s] + d
    for rows, x, acc in zip(groups, xs, accs):
        o_ref[rows, :] = x + _rms(acc, n2_ref[...])


def _merge_ffn(x2, pa, gb, mc, glog, wa, wb, wc, wo, n_mix, n1, wgu, wd, n2, layer, tm=512):
    t = x2.shape[0]
    row = lambda w: pl.BlockSpec((tm, w), lambda i: (i, 0))
    return pl.pallas_call(
        _merge_ffn_kernel,
        grid=(t // tm,),
        in_specs=[
            row(D_MODEL), row(POOL_WIDTH), row(GLA_VAL_WIDTH), row(MLA_VAL_WIDTH),
            row(N_BRANCH * D_MODEL),
            _layer_spec(layer, (POOL_WIDTH, D_MODEL)), _layer_spec(layer, (GLA_VAL_WIDTH, D_MODEL)),
            _layer_spec(layer, (MLA_VAL_WIDTH, D_MODEL)), _layer_spec(layer, (D_MODEL, D_MODEL)),
            _layer_spec(layer, (1, D_MODEL)), _layer_spec(layer, (1, D_MODEL)),
            _layer_spec(layer, (D_MODEL, D_FF), col=0), _layer_spec(layer, (D_MODEL, D_FF), col=1),
            _layer_spec(layer, (D_FF, D_MODEL)), _layer_spec(layer, (1, D_MODEL)),
        ],
        out_specs=row(D_MODEL),
        out_shape=jax.ShapeDtypeStruct((t, D_MODEL), F32),
        compiler_params=pltpu.CompilerParams(
            dimension_semantics=("arbitrary",), vmem_limit_bytes=MERGE_FFN_VMEM),
        name="merge_ffn",
    )(x2, pa, gb, mc, glog, wa, wb, wc, wo, n_mix, n1, wgu, wgu, wd, n2)


def _prep_weights(w_in, w_gla_a2, w_mla_uq, w_mla_ukv):
    sizes = (POOL_WIDTH, GLA_KEY_WIDTH, GLA_KEY_WIDTH, GLA_VAL_WIDTH, GLA_VAL_WIDTH,
             GLA_GATE_RANK, MLA_Q_RANK, MLA_KV_RANK, MLA_ROPE, N_BRANCH * D_MODEL)
    offs = [0]
    for s in sizes:
        offs.append(offs[-1] + s)
    assert offs[5] == IN_MAIN
    w_main = w_in[:, :, :IN_MAIN].astype(BF16)
    w_a1, w_cq, w_ckv, w_kr, w_gate = (w_in[:, :, offs[i]:offs[i + 1]] for i in range(5, 10))
    w_gate = w_gate.astype(BF16)
    d = w_in.shape[0]
    z16 = jnp.zeros((d, D_MODEL, 16), w_in.dtype)
    z48 = jnp.zeros((d, D_MODEL, 48), w_in.dtype)
    w_mid = jnp.concatenate(
        [w_cq, w_ckv, w_a1, z16, w_kr[..., :16], z48, w_kr[..., 16:], z16], axis=-1).astype(BF16)

    wa2 = jnp.pad(w_gla_a2, ((0, 0), (0, LANES - GLA_GATE_RANK), (0, 0))).astype(BF16)

    uq = w_mla_uq.reshape(d, MLA_Q_RANK, MLA_HEADS, MLA_QK)
    zq = jnp.zeros((d, MLA_Q_RANK, MLA_HEADS, 16), w_mla_uq.dtype)
    wq = jnp.concatenate(
        [uq[..., 0:32], uq[..., 64:80], zq, uq[..., 32:64], uq[..., 80:96], zq],
        axis=-1).reshape(d, MLA_Q_RANK, MLA_HEADS * HEAD_PAD).astype(BF16)

    ukv = w_mla_ukv.reshape(d, MLA_KV_RANK, MLA_HEADS, MLA_NOPE + MLA_V)
    zk = jnp.zeros((d, MLA_KV_RANK, MLA_HEADS, 32), w_mla_ukv.dtype)
    wk = jnp.concatenate(
        [ukv[..., 0:32], zk, ukv[..., 32:64], zk],
        axis=-1).reshape(d, MLA_KV_RANK, MLA_HEADS * HEAD_PAD).astype(BF16)
    wv = ukv[..., MLA_NOPE:].reshape(d, MLA_KV_RANK, MLA_VAL_WIDTH).astype(BF16)
    return w_main, w_mid, w_gate, wa2, wq, wk, wv


def kernel(x, positions, norm_pre_mix, norm_post_mix, norm_pre_ffn, norm_post_ffn, w_in, w_pool,
           pool_scale, w_a, w_gla_a2, b_gla_a, gla_norm, w_b, mla_q_norm, w_mla_uq, mla_kv_norm,
           w_mla_ukv, w_c, w_o, w_ffn_gu, w_ffn_down):
    batch, seq, d_model = x.shape
    t = batch * seq
    x2 = x.reshape(t, d_model)
    c_tab, s_tab = _rope_tables(positions)
    w_main, w_mid, w_gate, wa2, wq, wk, wv = _prep_weights(w_in, w_gla_a2, w_mla_uq, w_mla_ukv)
    w_pool_b = w_pool.astype(BF16)
    w_a_b, w_b_b, w_c_b, w_o_b = (w.astype(BF16) for w in (w_a, w_b, w_c, w_o))
    w_gu_b = w_ffn_gu.astype(BF16)
    w_d_b = w_ffn_down.astype(BF16)
    vec = lambda a: a.reshape(DEPTH, 1, -1)
    n_pre_mix, n_post_mix, n_pre_ffn, n_post_ffn = (
        vec(a) for a in (norm_pre_mix, norm_post_mix, norm_pre_ffn, norm_post_ffn))
    pool_scale, b_gla_a, gla_norm, mla_q_norm, mla_kv_norm = (
        vec(a) for a in (pool_scale, b_gla_a, gla_norm, mla_q_norm, mla_kv_norm))
    for l in range(DEPTH):
        gq, gk, gv, gr, misc, glog, pa, mq, mk, mv = _front(
            x2, n_pre_mix, w_main, w_mid, w_gate, c_tab, s_tab, mla_q_norm, mla_kv_norm,
            wq, wk, wv, w_pool_b, pool_scale, l, seq)
        gb = _gla(gq, gk, gv, gr, misc, wa2, b_gla_a, gla_norm, l, batch, seq)
        mc = _attention(mq, mk, mv, batch, seq)
        x2 = _merge_ffn(x2, pa, gb, mc, glog, w_a_b, w_b_b, w_c_b, w_o_b, n_post_mix,
                        n_pre_ffn, w_gu_b, w_d_b, n_post_ffn, l)
    return x2.reshape(batch, seq, d_model)
```

```python
import functools

import jax
import jax.numpy as jnp
import numpy as np
from jax import lax
from jax.experimental import pallas as pl
from jax.experimental.pallas import tpu as pltpu

F32 = jnp.float32
BF16 = jnp.bfloat16

D_MODEL = 1024
DEPTH = 4
POOL_WIDTH = 512
POOL_WINDOWS = (2, 4, 8, 16)
POOL_GROUP_DIM = 128
GLA_HEADS = 4
GLA_DK = 64
GLA_DV = 128
GLA_KEY_WIDTH = GLA_HEADS * GLA_DK
GLA_VAL_WIDTH = GLA_HEADS * GLA_DV
GLA_GATE_RANK = 16
GLA_GATE_TAU = 16.0
GLA_CHUNK = 64
GLA_SUPER = 256
MLA_HEADS = 8
MLA_Q_RANK = 384
MLA_KV_RANK = 256
MLA_NOPE = 64
MLA_ROPE = 32
MLA_V = 64
MLA_QK = MLA_NOPE + MLA_ROPE
MLA_VAL_WIDTH = MLA_HEADS * MLA_V
ROPE_BASE = 10000.0
N_BRANCH = 3
D_FF = 2816
EPS = 1e-6
LOG2E = 1.4426950408889634

LANES = 128
HEAD_PAD = 128
VMEM_LIMIT = 56 * 1024 * 1024

IN_MAIN = 2048
IN_MID = MLA_Q_RANK + MLA_KV_RANK + LANES
IN_GATE = N_BRANCH * D_MODEL


def _dot(a, b):
    return jnp.dot(a, b, preferred_element_type=F32)


def _dot_nt(a, b):
    return lax.dot_general(a, b, (((1,), (1,)), ((), ())), preferred_element_type=F32)


def _dot_tn(a, b):
    return lax.dot_general(a, b, (((0,), (0,)), ((), ())), preferred_element_type=F32)


def _rms(x, g):
    return x * lax.rsqrt(jnp.mean(x * x, axis=-1, keepdims=True) + EPS) * g


def _sigmoid(x):
    return 0.5 * jnp.tanh(0.5 * x) + 0.5


def _params(n_axes):
    return pltpu.CompilerParams(
        dimension_semantics=("arbitrary",) * n_axes, vmem_limit_bytes=VMEM_LIMIT)


def _layer_spec(layer, tail, col=0):
    idx = (layer,) + (0,) * (len(tail) - 1) + (col,)
    return pl.BlockSpec((None,) + tuple(tail), lambda *_: idx, pipeline_mode=pl.Buffered(1))


ROPE_HALF = MLA_ROPE // 2
TOK_PER_ROW = LANES // ROPE_HALF


def _split3(x):
    hi = x.astype(BF16)
    r1 = x - hi.astype(F32)
    mid = r1.astype(BF16)
    lo = (r1 - mid.astype(F32)).astype(BF16)
    return hi, mid, lo


def _rope_kernel(pos_ref, invf_ref, ec_ref, es_ref, base_ref, c_ref, s_ref):
    ang = pos_ref[...].astype(F32) * invf_ref[...]
    cos3 = _split3(jnp.cos(ang))
    sin3 = _split3(jnp.sin(ang))
    for j in range(TOK_PER_ROW):
        ec, es = ec_ref[j], es_ref[j]
        c_ref[:, j, :] = sum(_dot(t, ec) for t in cos3) + base_ref[...]
        s_ref[:, j, :] = sum(_dot(t, es) for t in sin3)


def _rope_tables(positions):
    t = positions.size
    inv_freq = ROPE_BASE ** (-jnp.arange(0, MLA_ROPE, 2, dtype=F32) / MLA_ROPE)
    rows = t // TOK_PER_ROW
    pos_d = jnp.repeat(positions.reshape(t), ROPE_HALF).reshape(rows, LANES)
    invf_d = jnp.tile(inv_freq, TOK_PER_ROW).reshape(1, LANES)
    ec = np.zeros((TOK_PER_ROW, LANES, HEAD_PAD), np.float32)
    es = np.zeros((TOK_PER_ROW, LANES, HEAD_PAD), np.float32)
    for j in range(TOK_PER_ROW):
        for i in range(ROPE_HALF):
            ec[j, j * ROPE_HALF + i, 32 + i] = 1.0
            ec[j, j * ROPE_HALF + i, 96 + i] = 1.0
            es[j, j * ROPE_HALF + i, 32 + i] = -1.0
            es[j, j * ROPE_HALF + i, 96 + i] = 1.0
    base = np.zeros((1, HEAD_PAD), np.float32)
    base[0, 0:32] = 1.0
    base[0, 64:96] = 1.0
    c3, s3 = pl.pallas_call(
        _rope_kernel,
        out_shape=(jax.ShapeDtypeStruct((rows, TOK_PER_ROW, HEAD_PAD), F32),) * 2,
        compiler_params=pltpu.CompilerParams(vmem_limit_bytes=VMEM_LIMIT),
        name="rope_tables",
    )(pos_d, invf_d, jnp.asarray(ec, BF16), jnp.asarray(es, BF16), jnp.asarray(base))
    return c3.reshape(t, HEAD_PAD), s3.reshape(t, HEAD_PAD)


POOL_HALO = 16


def _front_kernel(x_ref, g_ref, wm_ref, wd_ref, wg_ref, c_ref, s_ref, qn_ref, kvn_ref,
                  wq_ref, wk_ref, wv_ref, wp_ref, ps_ref,
                  gq_ref, gk_ref, gv_ref, gr_ref, misc_ref, glog_ref, pa_ref, q_out, k_out, v_out,
                  halo_ref, *, tiles_per_seq):
    tm = x_ref.shape[0]
    tile_in_seq = pl.program_id(0) % tiles_per_seq

    @pl.when(tile_in_seq == 0)
    def _():
        halo_ref[...] = jnp.zeros_like(halo_ref)

    h = _rms(x_ref[...], g_ref[...]).astype(BF16)

    y_mid = _dot(h, wd_ref[...])
    cq = y_mid[:, :MLA_Q_RANK]
    ckv = y_mid[:, MLA_Q_RANK:MLA_Q_RANK + MLA_KV_RANK]
    misc = y_mid[:, MLA_Q_RANK + MLA_KV_RANK:]
    misc_ref[...] = misc.astype(BF16)

    y0 = _dot(h, wm_ref[:, :2 * POOL_WIDTH])
    u = y0[:, :POOL_WIDTH]
    gq_ref[...] = y0[:, POOL_WIDTH:POOL_WIDTH + GLA_KEY_WIDTH].astype(BF16)
    gk_ref[...] = y0[:, POOL_WIDTH + GLA_KEY_WIDTH:].astype(BF16)
    y1 = _dot(h, wm_ref[:, 2 * POOL_WIDTH:])
    gv_ref[...] = y1[:, :GLA_VAL_WIDTH].astype(BF16)
    gr_ref[...] = y1[:, GLA_VAL_WIDTH:].astype(BF16)

    cqn = _rms(cq, qn_ref[...]).astype(BF16)
    qf = _dot(cqn, wq_ref[...])
    ckvn = _rms(ckv, kvn_ref[...]).astype(BF16)
    kf = _dot(ckvn, wk_ref[...])
    v_out[...] = _dot(ckvn, wv_ref[...]).astype(BF16)

    ext = jnp.concatenate([halo_ref[...], u], axis=0)
    halo_ref[...] = u[tm - POOL_HALO:, :]
    pos = tile_in_seq * tm + lax.broadcasted_iota(jnp.int32, (tm, POOL_GROUP_DIM), 0)
    for g, w in enumerate(POOL_WINDOWS):
        cols = slice(g * POOL_GROUP_DIM, (g + 1) * POOL_GROUP_DIM)
        s = ext[:, cols]
        sh = 1
        while sh < w:
            s = s + pltpu.roll(s, sh, axis=0)
            sh *= 2
        cnt = jnp.minimum(pos + 1, w).astype(F32)
        diff = s[POOL_HALO:] / cnt - u[:, cols]
        pa_ref[:, cols] = (_dot(diff.astype(BF16), wp_ref[g]) * ps_ref[:, cols]).astype(BF16)

    for a in range(0, IN_GATE, 1024):
        glog_ref[:, a:a + 1024] = _dot(h, wg_ref[:, a:a + 1024]).astype(BF16)

    c_tab = c_ref[...]
    s_tab = s_ref[...]
    scale = MLA_QK ** -0.5 * LOG2E
    lane = lax.broadcasted_iota(jnp.int32, (1, HEAD_PAD), 1)
    rope_lane = ((lane >= 32) & (lane < 48)) | ((lane >= 96) & (lane < 112))
    kr = jnp.where(rope_lane, misc * c_tab + pltpu.roll(misc, 64, axis=1) * s_tab, 0.0)
    for hd in range(MLA_HEADS):
        cols = slice(hd * HEAD_PAD, (hd + 1) * HEAD_PAD)
        xq = qf[:, cols]
        q_out[:, cols] = ((xq * c_tab + pltpu.roll(xq, 64, axis=1) * s_tab) * scale).astype(BF16)
        k_out[:, cols] = (kf[:, cols] + kr).astype(BF16)


def _front(x2, g, w_main, w_mid, w_gate, c_tab, s_tab, qn, kvn, wq, wk, wv, wp, ps, layer, seq,
           tm=512):
    t = x2.shape[0]
    hw = MLA_HEADS * HEAD_PAD
    row = lambda w: pl.BlockSpec((tm, w), lambda i: (i, 0))
    outs = ((GLA_KEY_WIDTH, BF16), (GLA_KEY_WIDTH, BF16), (GLA_VAL_WIDTH, BF16),
            (GLA_VAL_WIDTH, BF16), (LANES, BF16), (IN_GATE, BF16), (POOL_WIDTH, BF16),
            (hw, BF16), (hw, BF16), (MLA_VAL_WIDTH, BF16))
    return pl.pallas_call(
        functools.partial(_front_kernel, tiles_per_seq=seq // tm),
        grid=(t // tm,),
        in_specs=[
            row(D_MODEL),
            _layer_spec(layer, (1, D_MODEL)),
            _layer_spec(layer, (D_MODEL, IN_MAIN)),
            _layer_spec(layer, (D_MODEL, IN_MID)),
            _layer_spec(layer, (D_MODEL, IN_GATE)),
            row(HEAD_PAD), row(HEAD_PAD),
            _layer_spec(layer, (1, MLA_Q_RANK)), _layer_spec(layer, (1, MLA_KV_RANK)),
            _layer_spec(layer, (MLA_Q_RANK, hw)), _layer_spec(layer, (MLA_KV_RANK, hw)),
            _layer_spec(layer, (MLA_KV_RANK, MLA_VAL_WIDTH)),
            _layer_spec(layer, wp.shape[1:]),
            _layer_spec(layer, (1, POOL_WIDTH)),
        ],
        out_specs=tuple(row(w) for w, _ in outs),
        out_shape=tuple(jax.ShapeDtypeStruct((t, w), dt) for w, dt in outs),
        scratch_shapes=[pltpu.VMEM((POOL_HALO, POOL_WIDTH), F32)],
        compiler_params=_params(1),
        name="front",
    )(x2, g, w_main, w_mid, w_gate, c_tab, s_tab, qn, kvn, wq, wk, wv, wp, ps)


def _gla_kernel(q_ref, k_ref, v_ref, r_ref, misc_ref, wa2_ref, ba_ref, gn_ref, o_ref):
    s_len, key_w = q_ref.shape
    c = GLA_CHUNK
    sc = GLA_SUPER
    pw = 2 * GLA_DK
    row = lax.broadcasted_iota(jnp.int32, (sc, sc), 0)
    col = lax.broadcasted_iota(jnp.int32, (sc, sc), 1)
    same_chunk = (row // c) == (col // c)
    tril = same_chunk & (row >= col)
    tri_bf = tril.astype(BF16)
    lane = lax.broadcasted_iota(jnp.int32, (1, pw), 1)
    head_lanes = [(lane >= hh * GLA_DK) & (lane < (hh + 1) * GLA_DK) for hh in range(2)]
    blk = lax.broadcasted_iota(jnp.int32, (sc, pw), 0) // c
    nsub = sc // c

    def spread(x):
        return jnp.concatenate(
            [jnp.where(blk == j, x, 0.0).astype(BF16) for j in range(nsub)], axis=1)

    blocks = [slice(n * sc, (n + 1) * sc) for n in range(s_len // sc)]
    pairs = [slice(p * pw, (p + 1) * pw) for p in range(key_w // pw)]
    vcols = lambda p, hh: slice((2 * p + hh) * GLA_DV, (2 * p + hh + 1) * GLA_DV)

    z = _dot(misc_ref[...], wa2_ref[...]) + ba_ref[...]
    la = (jnp.minimum(z, 0.0) - jnp.log1p(jnp.exp(-jnp.abs(z)))) * (1.0 / GLA_GATE_TAU)
    la3 = jnp.concatenate(_split3(la), axis=1)
    cum3 = [_dot(tri_bf, la3[rows]) for rows in blocks]

    jobs = []
    for rows, c3 in zip(blocks, cum3):
        cum_all = c3[:, :key_w] + c3[:, key_w:2 * key_w] + c3[:, 2 * key_w:]
        for p, kc in enumerate(pairs):
            cum = cum_all[:, kc]
            ends = [cum[(j + 1) * c - 1:(j + 1) * c, :] for j in range(nsub)]
            tot = jnp.concatenate([jnp.broadcast_to(r, (c, pw)) for r in ends], axis=0)
            qf = q_ref[rows, kc].astype(F32) * (GLA_DK ** -0.5)
            kf = k_ref[rows, kc].astype(F32)
            jobs.append((p, rows, qf * jnp.exp(cum), (kf * jnp.exp(-cum)).astype(BF16),
                         kf * jnp.exp(tot - cum), ends))

    upd = [sum(_dot_tn(v_ref[rows, vcols(p, hh)], spread(jnp.where(head_lanes[hh], ke, 0.0)))
               for hh in range(2)) for p, rows, _, _, ke, _ in jobs]
    att = [[jnp.where(tril, _dot_nt(jnp.where(head_lanes[hh], qd, 0.0).astype(BF16), ki),
                      0.0).astype(BF16) for hh in range(2)] for _, _, qd, ki, _, _ in jobs]

    state = [jnp.zeros((GLA_DV, pw), F32) for _ in pairs]
    st_cat = []
    for (p, _, _, _, _, ends), u in zip(jobs, upd):
        sts = []
        for j in range(nsub):
            sts.append(state[p].astype(BF16))
            state[p] = state[p] * jnp.exp(ends[j]) + u[:, j * LANES:(j + 1) * LANES]
        st_cat.append(jnp.concatenate(sts, axis=1))

    for (p, rows, qd, _, _, _), a, sc_n in zip(jobs, att, st_cat):
        for hh in range(2):
            vc = vcols(p, hh)
            o = _dot(a[hh], v_ref[rows, vc])
            o = o + _dot_nt(spread(jnp.where(head_lanes[hh], qd, 0.0)), sc_n)
            o = _rms(o, gn_ref[:, vc])
            rr = r_ref[rows, vc].astype(F32)
            o_ref[rows, vc] = (o * (rr * _sigmoid(rr))).astype(BF16)


def _gla(q, k, v, r, misc, wa2, ba, gn, layer, batch, seq):
    t = q.shape[0]
    seq_block = lambda w: pl.BlockSpec((seq, w), lambda b: (b, 0))
    return pl.pallas_call(
        _gla_kernel,
        grid=(batch,),
        in_specs=[
            seq_block(GLA_KEY_WIDTH), seq_block(GLA_KEY_WIDTH),
            seq_block(GLA_VAL_WIDTH), seq_block(GLA_VAL_WIDTH), seq_block(LANES),
            _layer_spec(layer, (LANES, GLA_KEY_WIDTH)),
            _layer_spec(layer, (1, GLA_KEY_WIDTH)),
            _layer_spec(layer, (1, GLA_VAL_WIDTH)),
        ],
        out_specs=seq_block(GLA_VAL_WIDTH),
        out_shape=jax.ShapeDtypeStruct((t, GLA_VAL_WIDTH), BF16),
        compiler_params=_params(1),
        name="gla",
    )(q, k, v, r, misc, wa2, ba, gn)


ATT_TQ = 512
ATT_HALF = ATT_TQ // 2
ATT_AHEAD = 2


def _attn_scores(q_ref, k_ref, qi, cols):
    tq, hf = ATT_TQ, ATT_HALF
    r0 = qi * tq
    q = q_ref[r0:r0 + tq, cols]
    k_d = k_ref[r0:r0 + tq, cols]
    d0 = _dot_nt(k_d[:hf], q[:hf])
    d1 = _dot_nt(k_d, q[hf:])
    s_off = _dot_nt(k_ref[0:r0, cols], q) if qi > 0 else None
    return d0, d1, s_off


def _attn_causal_masks():
    tq, hf = ATT_TQ, ATT_HALF
    key0 = lax.broadcasted_iota(jnp.int32, (hf, hf), 0)
    qry0 = lax.broadcasted_iota(jnp.int32, (hf, hf), 1)
    key1 = lax.broadcasted_iota(jnp.int32, (tq, hf), 0)
    qry1 = lax.broadcasted_iota(jnp.int32, (tq, hf), 1)
    return key0 <= qry0, key1 <= qry1 + hf


def _attn_finish(vt, ones_row, masks, qi, scores):
    tq, hf = ATT_TQ, ATT_HALF
    r0 = qi * tq
    neg = jnp.finfo(F32).min
    d0, d1, s_off = scores
    d0 = jnp.where(masks[0], d0, neg)
    d1 = jnp.where(masks[1], d1, neg)
    m0 = jnp.max(d0, axis=0, keepdims=True)
    m1 = jnp.max(d1, axis=0, keepdims=True)
    if s_off is not None:
        m0 = jnp.maximum(m0, jnp.max(s_off[:, :hf], axis=0, keepdims=True))
        m1 = jnp.maximum(m1, jnp.max(s_off[:, hf:], axis=0, keepdims=True))
        p_off0 = jnp.exp2(s_off[:, :hf] - m0)
        p_off1 = jnp.exp2(s_off[:, hf:] - m1)
        p_off = jnp.concatenate([p_off0.astype(BF16), p_off1.astype(BF16)], axis=1)
        o_off = _dot(vt[:, 0:r0], p_off)
    p0 = jnp.exp2(d0 - m0)
    p1 = jnp.exp2(d1 - m1)
    o0 = _dot(vt[:, r0:r0 + hf], p0.astype(BF16))
    o1 = _dot(vt[:, r0:r0 + tq], p1.astype(BF16))
    if s_off is not None:
        o0 = o0 + o_off[:, :hf]
        o1 = o1 + o_off[:, hf:]
    is_ones = lax.broadcasted_iota(jnp.int32, (HEAD_PAD, hf), 0) == ones_row
    return tuple(jnp.where(is_ones, 0.0, o / o[ones_row:ones_row + 1, :]) for o in (o0, o1))


def _attn_kernel(q_ref, k_ref, v_ref, o_ref):
    tq, hf = ATT_TQ, ATT_HALF
    n_pair = v_ref.shape[1] // HEAD_PAD
    head_cols = lambda p, hh: slice((2 * p + hh) * HEAD_PAD, (2 * p + hh + 1) * HEAD_PAD)
    masks = _attn_causal_masks()
    ones_rows = (MLA_V, 0)
    vt_head = []
    for p in range(n_pair):
        vt_pair = v_ref[:, p * HEAD_PAD:(p + 1) * HEAD_PAD].astype(F32).T
        vrow = lax.broadcasted_iota(jnp.int32, vt_pair.shape, 0)
        vt_head.append((
            jnp.where(vrow < MLA_V, vt_pair,
                      jnp.where(vrow == ones_rows[0], 1.0, 0.0)).astype(BF16),
            jnp.where(vrow >= MLA_V, vt_pair,
                      jnp.where(vrow == ones_rows[1], 1.0, 0.0)).astype(BF16)))
    units = [(p, qi, hh) for p in range(n_pair)
             for qi in reversed(range(q_ref.shape[0] // tq)) for hh in range(2)]
    scores_of = lambda u: _attn_scores(q_ref, k_ref, u[1], head_cols(u[0], u[2]))
    pending = [scores_of(u) for u in units[:ATT_AHEAD]]
    prev = None
    for i, (p, qi, hh) in enumerate(units):
        if i + ATT_AHEAD < len(units):
            pending.append(scores_of(units[i + ATT_AHEAD]))
        cur = _attn_finish(vt_head[p][hh], ones_rows[hh], masks, qi, pending.pop(0))
        if hh == 0:
            prev = cur
        else:
            ocols = slice(p * HEAD_PAD, (p + 1) * HEAD_PAD)
            o_ref[qi * tq:qi * tq + hf, ocols] = (prev[0] + cur[0]).T.astype(BF16)
            o_ref[qi * tq + hf:(qi + 1) * tq, ocols] = (prev[1] + cur[1]).T.astype(BF16)


def _attention(q, k, v, batch, seq):
    t = q.shape[0]
    seq_block = lambda a: pl.BlockSpec((seq, a.shape[1]), lambda b: (b, 0))
    return pl.pallas_call(
        _attn_kernel,
        grid=(batch,),
        in_specs=[seq_block(q), seq_block(k), seq_block(v)],
        out_specs=pl.BlockSpec((seq, MLA_VAL_WIDTH), lambda b: (b, 0)),
        out_shape=jax.ShapeDtypeStruct((t, MLA_VAL_WIDTH), BF16),
        compiler_params=_params(1),
        name="attention",
    )(q, k, v)


FFN_CHUNKS = ((0, 1024), (1024, 2048), (2048, D_FF))
MERGE_FFN_VMEM = 60 * 1024 * 1024
MERGE_FFN_SUB = 2


def _merge_ffn_kernel(x_ref, pa_ref, gb_ref, mc_ref, gl_ref, wa_ref, wb_ref, wc_ref, wo_ref,
                      n_mix_ref, n1_ref, wg_ref, wu_ref, wd_ref, n2_ref, o_ref):
    tm = x_ref.shape[0]
    sub = tm // MERGE_FFN_SUB
    groups = [slice(s * sub, (s + 1) * sub) for s in range(MERGE_FFN_SUB)]
    branches = ((pa_ref, wa_ref), (gb_ref, wb_ref), (mc_ref, wc_ref))

    merged = []
    for rows in groups:
        m = None
        for i, (b_ref, w_ref) in enumerate(branches):
            y = _dot(b_ref[rows, :], w_ref[...])
            gate = _sigmoid(gl_ref[rows, i * D_MODEL:(i + 1) * D_MODEL].astype(F32))
            m = gate * y if m is None else m + gate * y
        merged.append(m.astype(BF16))
    xs, hs = [], []
    for rows, m in zip(groups, merged):
        x = x_ref[rows, :] + _rms(_dot(m, wo_ref[...]), n_mix_ref[...])
        xs.append(x)
        hs.append(_rms(x, n1_ref[...]).astype(BF16))
    accs = [None] * len(groups)
    for a, b in FFN_CHUNKS:
        gu = [(_dot(h, wg_ref[:, a:b]), _dot(h, wu_ref[:, a:b])) for h in hs]
        for s, (g, u) in enumerate(gu):
            d = _dot((g * _sigmoid(g) * u).astype(BF16), wd_ref[a:b, :])
            accs[s] = d if accs[s] is None else accs[s] + d
    for rows, x, acc in zip(groups, xs, accs):
        o_ref[rows, :] = x + _rms(acc, n2_ref[...])


def _merge_ffn(x2, pa, gb, mc, glog, wa, wb, wc, wo, n_mix, n1, wgu, wd, n2, layer, tm=512):
    t = x2.shape[0]
    row = lambda w: pl.BlockSpec((tm, w), lambda i: (i, 0))
    return pl.pallas_call(
        _merge_ffn_kernel,
        grid=(t // tm,),
        in_specs=[
            row(D_MODEL), row(POOL_WIDTH), row(GLA_VAL_WIDTH), row(MLA_VAL_WIDTH),
            row(N_BRANCH * D_MODEL),
            _layer_spec(layer, (POOL_WIDTH, D_MODEL)), _layer_spec(layer, (GLA_VAL_WIDTH, D_MODEL)),
            _layer_spec(layer, (MLA_VAL_WIDTH, D_MODEL)), _layer_spec(layer, (D_MODEL, D_MODEL)),
            _layer_spec(layer, (1, D_MODEL)), _layer_spec(layer, (1, D_MODEL)),
            _layer_spec(layer, (D_MODEL, D_FF), col=0), _layer_spec(layer, (D_MODEL, D_FF), col=1),
            _layer_spec(layer, (D_FF, D_MODEL)), _layer_spec(layer, (1, D_MODEL)),
        ],
        out_specs=row(D_MODEL),
        out_shape=jax.ShapeDtypeStruct((t, D_MODEL), F32),
        compiler_params=pltpu.CompilerParams(
            dimension_semantics=("arbitrary",), vmem_limit_bytes=MERGE_FFN_VMEM),
        name="merge_ffn",
    )(x2, pa, gb, mc, glog, wa, wb, wc, wo, n_mix, n1, wgu, wgu, wd, n2)


def _prep_weights(w_in, w_gla_a2, w_mla_uq, w_mla_ukv):
    sizes = (POOL_WIDTH, GLA_KEY_WIDTH, GLA_KEY_WIDTH, GLA_VAL_WIDTH, GLA_VAL_WIDTH,
             GLA_GATE_RANK, MLA_Q_RANK, MLA_KV_RANK, MLA_ROPE, N_BRANCH * D_MODEL)
    offs = [0]
    for s in sizes:
        offs.append(offs[-1] + s)
    assert offs[5] == IN_MAIN
    w_main = w_in[:, :, :IN_MAIN].astype(BF16)
    w_a1, w_cq, w_ckv, w_kr, w_gate = (w_in[:, :, offs[i]:offs[i + 1]] for i in range(5, 10))
    w_gate = w_gate.astype(BF16)
    d = w_in.shape[0]
    z16 = jnp.zeros((d, D_MODEL, 16), w_in.dtype)
    z48 = jnp.zeros((d, D_MODEL, 48), w_in.dtype)
    w_mid = jnp.concatenate(
        [w_cq, w_ckv, w_a1, z16, w_kr[..., :16], z48, w_kr[..., 16:], z16], axis=-1).astype(BF16)

    wa2 = jnp.pad(w_gla_a2, ((0, 0), (0, LANES - GLA_GATE_RANK), (0, 0))).astype(BF16)

    uq = w_mla_uq.reshape(d, MLA_Q_RANK, MLA_HEADS, MLA_QK)
    zq = jnp.zeros((d, MLA_Q_RANK, MLA_HEADS, 16), w_mla_uq.dtype)
    wq = jnp.concatenate(
        [uq[..., 0:32], uq[..., 64:80], zq, uq[..., 32:64], uq[..., 80:96], zq],
        axis=-1).reshape(d, MLA_Q_RANK, MLA_HEADS * HEAD_PAD).astype(BF16)

    ukv = w_mla_ukv.reshape(d, MLA_KV_RANK, MLA_HEADS, MLA_NOPE + MLA_V)
    zk = jnp.zeros((d, MLA_KV_RANK, MLA_HEADS, 32), w_mla_ukv.dtype)
    wk = jnp.concatenate(
        [ukv[..., 0:32], zk, ukv[..., 32:64], zk],
        axis=-1).reshape(d, MLA_KV_RANK, MLA_HEADS * HEAD_PAD).astype(BF16)
    wv = ukv[..., MLA_NOPE:].reshape(d, MLA_KV_RANK, MLA_VAL_WIDTH).astype(BF16)
    return w_main, w_mid, w_gate, wa2, wq, wk, wv


def kernel(x, positions, norm_pre_mix, norm_post_mix, norm_pre_ffn, norm_post_ffn, w_in, w_pool,
           pool_scale, w_a, w_gla_a2, b_gla_a, gla_norm, w_b, mla_q_norm, w_mla_uq, mla_kv_norm,
           w_mla_ukv, w_c, w_o, w_ffn_gu, w_ffn_down):
    batch, seq, d_model = x.shape
    t = batch * seq
    x2 = x.reshape(t, d_model)
    c_tab, s_tab = _rope_tables(positions)
    w_main, w_mid, w_gate, wa2, wq, wk, wv = _prep_weights(w_in, w_gla_a2, w_mla_uq, w_mla_ukv)
    w_pool_b = w_pool.astype(BF16)
    w_a_b, w_b_b, w_c_b, w_o_b = (w.astype(BF16) for w in (w_a, w_b, w_c, w_o))
    w_gu_b = w_ffn_gu.astype(BF16)
    w_d_b = w_ffn_down.astype(BF16)
    vec = lambda a: a.reshape(DEPTH, 1, -1)
    n_pre_mix, n_post_mix, n_pre_ffn, n_post_ffn = (
        vec(a) for a in (norm_pre_mix, norm_post_mix, norm_pre_ffn, norm_post_ffn))
    pool_scale, b_gla_a, gla_norm, mla_q_norm, mla_kv_norm = (
        vec(a) for a in (pool_scale, b_gla_a, gla_norm, mla_q_norm, mla_kv_norm))
    for l in range(DEPTH):
        gq, gk, gv, gr, misc, glog, pa, mq, mk, mv = _front(
            x2, n_pre_mix, w_main, w_mid, w_gate, c_tab, s_tab, mla_q_norm, mla_kv_norm,
            wq, wk, wv, w_pool_b, pool_scale, l, seq)
        gb = _gla(gq, gk, gv, gr, misc, wa2, b_gla_a, gla_norm, l, batch, seq)
        mc = _attention(mq, mk, mv, batch, seq)
        x2 = _merge_ffn(x2, pa, gb, mc, glog, w_a_b, w_b_b, w_c_b, w_o_b, n_post_mix,
                        n_pre_ffn, w_gu_b, w_d_b, n_post_ffn, l)
    return x2.reshape(batch, seq, d_model)
```

```python
import functools

import jax
import jax.numpy as jnp
import numpy as np
from jax import lax
from jax.experimental import pallas as pl
from jax.experimental.pallas import tpu as pltpu

F32 = jnp.float32
BF16 = jnp.bfloat16

D_MODEL = 1024
DEPTH = 4
POOL_WIDTH = 512
POOL_WINDOWS = (2, 4, 8, 16)
POOL_GROUP_DIM = 128
GLA_HEADS = 4
GLA_DK = 64
GLA_DV = 128
GLA_KEY_WIDTH = GLA_HEADS * GLA_DK
GLA_VAL_WIDTH = GLA_HEADS * GLA_DV
GLA_GATE_RANK = 16
GLA_GATE_TAU = 16.0
GLA_CHUNK = 64
GLA_SUPER = 256
MLA_HEADS = 8
MLA_Q_RANK = 384
MLA_KV_RANK = 256
MLA_NOPE = 64
MLA_ROPE = 32
MLA_V = 64
MLA_QK = MLA_NOPE + MLA_ROPE
MLA_VAL_WIDTH = MLA_HEADS * MLA_V
ROPE_BASE = 10000.0
N_BRANCH = 3
D_FF = 2816
EPS = 1e-6
LOG2E = 1.4426950408889634

LANES = 128
HEAD_PAD = 128
VMEM_LIMIT = 56 * 1024 * 1024

IN_MAIN = 2048
IN_MID = MLA_Q_RANK + MLA_KV_RANK + LANES
IN_GATE = N_BRANCH * D_MODEL


def _dot(a, b):
    return jnp.dot(a, b, preferred_element_type=F32)


def _dot_nt(a, b):
    return lax.dot_general(a, b, (((1,), (1,)), ((), ())), preferred_element_type=F32)


def _dot_tn(a, b):
    return lax.dot_general(a, b, (((0,), (0,)), ((), ())), preferred_element_type=F32)


def _rms(x, g):
    return x * lax.rsqrt(jnp.mean(x * x, axis=-1, keepdims=True) + EPS) * g


def _sigmoid(x):
    return 0.5 * jnp.tanh(0.5 * x) + 0.5


def _params(n_axes):
    return pltpu.CompilerParams(
        dimension_semantics=("arbitrary",) * n_axes, vmem_limit_bytes=VMEM_LIMIT)


def _layer_spec(layer, tail, col=0):
    idx = (layer,) + (0,) * (len(tail) - 1) + (col,)
    return pl.BlockSpec((None,) + tuple(tail), lambda *_: idx, pipeline_mode=pl.Buffered(1))


ROPE_HALF = MLA_ROPE // 2
TOK_PER_ROW = LANES // ROPE_HALF


def _split3(x):
    hi = x.astype(BF16)
    r1 = x - hi.astype(F32)
    mid = r1.astype(BF16)
    lo = (r1 - mid.astype(F32)).astype(BF16)
    return hi, mid, lo


def _rope_kernel(pos_ref, invf_ref, ec_ref, es_ref, base_ref, c_ref, s_ref):
    ang = pos_ref[...].astype(F32) * invf_ref[...]
    cos3 = _split3(jnp.cos(ang))
    sin3 = _split3(jnp.sin(ang))
    for j in range(TOK_PER_ROW):
        ec, es = ec_ref[j], es_ref[j]
        c_ref[:, j, :] = sum(_dot(t, ec) for t in cos3) + base_ref[...]
        s_ref[:, j, :] = sum(_dot(t, es) for t in sin3)


def _rope_tables(positions):
    t = positions.size
    inv_freq = ROPE_BASE ** (-jnp.arange(0, MLA_ROPE, 2, dtype=F32) / MLA_ROPE)
    rows = t // TOK_PER_ROW
    pos_d = jnp.repeat(positions.reshape(t), ROPE_HALF).reshape(rows, LANES)
    invf_d = jnp.tile(inv_freq, TOK_PER_ROW).reshape(1, LANES)
    ec = np.zeros((TOK_PER_ROW, LANES, HEAD_PAD), np.float32)
    es = np.zeros((TOK_PER_ROW, LANES, HEAD_PAD), np.float32)
    for j in range(TOK_PER_ROW):
        for i in range(ROPE_HALF):
            ec[j, j * ROPE_HALF + i, 32 + i] = 1.0
            ec[j, j * ROPE_HALF + i, 96 + i] = 1.0
            es[j, j * ROPE_HALF + i, 32 + i] = -1.0
            es[j, j * ROPE_HALF + i, 96 + i] = 1.0
    base = np.zeros((1, HEAD_PAD), np.float32)
    base[0, 0:32] = 1.0
    base[0, 64:96] = 1.0
    c3, s3 = pl.pallas_call(
        _rope_kernel,
        out_shape=(jax.ShapeDtypeStruct((rows, TOK_PER_ROW, HEAD_PAD), F32),) * 2,
        compiler_params=pltpu.CompilerParams(vmem_limit_bytes=VMEM_LIMIT),
        name="rope_tables",
    )(pos_d, invf_d, jnp.asarray(ec, BF16), jnp.asarray(es, BF16), jnp.asarray(base))
    return c3.reshape(t, HEAD_PAD), s3.reshape(t, HEAD_PAD)


POOL_HALO = 16
FRONT_SUB = 2


def _front_kernel(x_ref, g_ref, wm_ref, wd_ref, wg_ref, c_ref, s_ref, qn_ref, kvn_ref,
                  wq_ref, wk_ref, wv_ref, wp_ref, ps_ref,
                  gq_ref, gk_ref, gv_ref, gr_ref, misc_ref, glog_ref, pa_ref, q_out, k_out, v_out,
                  halo_ref, *, tiles_per_seq):
    tm = x_ref.shape[0]
    tile_in_seq = pl.program_id(0) % tiles_per_seq

    @pl.when(tile_in_seq == 0)
    def _():
        halo_ref[...] = jnp.zeros_like(halo_ref)

    sub = tm // FRONT_SUB
    groups = [slice(s * sub, (s + 1) * sub) for s in range(FRONT_SUB)]
    hs = [_rms(x_ref[rows, :], g_ref[...]).astype(BF16) for rows in groups]

    mids = [_dot(h, wd_ref[...]) for h in hs]
    for rows, y_mid in zip(groups, mids):
        misc_ref[rows, :] = y_mid[:, MLA_Q_RANK + MLA_KV_RANK:].astype(BF16)

    y0s = [_dot(h, wm_ref[:, :2 * POOL_WIDTH]) for h in hs]
    y1s = [_dot(h, wm_ref[:, 2 * POOL_WIDTH:]) for h in hs]
    for rows, y0, y1 in zip(groups, y0s, y1s):
        gq_ref[rows, :] = y0[:, POOL_WIDTH:POOL_WIDTH + GLA_KEY_WIDTH].astype(BF16)
        gk_ref[rows, :] = y0[:, POOL_WIDTH + GLA_KEY_WIDTH:].astype(BF16)
        gv_ref[rows, :] = y1[:, :GLA_VAL_WIDTH].astype(BF16)
        gr_ref[rows, :] = y1[:, GLA_VAL_WIDTH:].astype(BF16)

    qfs, kfs = [], []
    for rows, y_mid in zip(groups, mids):
        cqn = _rms(y_mid[:, :MLA_Q_RANK], qn_ref[...]).astype(BF16)
        ckvn = _rms(y_mid[:, MLA_Q_RANK:MLA_Q_RANK + MLA_KV_RANK], kvn_ref[...]).astype(BF16)
        qfs.append(_dot(cqn, wq_ref[...]))
        kfs.append(_dot(ckvn, wk_ref[...]))
        v_out[rows, :] = _dot(ckvn, wv_ref[...]).astype(BF16)

    u = jnp.concatenate([y0[:, :POOL_WIDTH] for y0 in y0s], axis=0)
    ext = jnp.concatenate([halo_ref[...], u], axis=0)
    halo_ref[...] = u[tm - POOL_HALO:, :]
    pos = tile_in_seq * tm + lax.broadcasted_iota(jnp.int32, (tm, POOL_GROUP_DIM), 0)
    for g, w in enumerate(POOL_WINDOWS):
        cols = slice(g * POOL_GROUP_DIM, (g + 1) * POOL_GROUP_DIM)
        s = ext[:, cols]
        sh = 1
        while sh < w:
            s = s + pltpu.roll(s, sh, axis=0)
            sh *= 2
        cnt = jnp.minimum(pos + 1, w).astype(F32)
        diff = s[POOL_HALO:] / cnt - u[:, cols]
        pa_ref[:, cols] = (_dot(diff.astype(BF16), wp_ref[g]) * ps_ref[:, cols]).astype(BF16)

    for a in range(0, IN_GATE, 1024):
        for rows, h in zip(groups, hs):
            glog_ref[rows, a:a + 1024] = _dot(h, wg_ref[:, a:a + 1024]).astype(BF16)

    scale = MLA_QK ** -0.5 * LOG2E
    lane = lax.broadcasted_iota(jnp.int32, (1, HEAD_PAD), 1)
    rope_lane = ((lane >= 32) & (lane < 48)) | ((lane >= 96) & (lane < 112))
    for rows, y_mid, qf, kf in zip(groups, mids, qfs, kfs):
        c_tab = c_ref[rows, :]
        s_tab = s_ref[rows, :]
        misc = y_mid[:, MLA_Q_RANK + MLA_KV_RANK:]
        kr = jnp.where(rope_lane, misc * c_tab + pltpu.roll(misc, 64, axis=1) * s_tab, 0.0)
        for hd in range(MLA_HEADS):
            cols = slice(hd * HEAD_PAD, (hd + 1) * HEAD_PAD)
            xq = qf[:, cols]
            q_out[rows, cols] = (
                (xq * c_tab + pltpu.roll(xq, 64, axis=1) * s_tab) * scale).astype(BF16)
            k_out[rows, cols] = (kf[:, cols] + kr).astype(BF16)


def _front(x2, g, w_main, w_mid, w_gate, c_tab, s_tab, qn, kvn, wq, wk, wv, wp, ps, layer, seq,
           tm=512):
    t = x2.shape[0]
    hw = MLA_HEADS * HEAD_PAD
    row = lambda w: pl.BlockSpec((tm, w), lambda i: (i, 0))
    outs = ((GLA_KEY_WIDTH, BF16), (GLA_KEY_WIDTH, BF16), (GLA_VAL_WIDTH, BF16),
            (GLA_VAL_WIDTH, BF16), (LANES, BF16), (IN_GATE, BF16), (POOL_WIDTH, BF16),
            (hw, BF16), (hw, BF16), (MLA_VAL_WIDTH, BF16))
    return pl.pallas_call(
        functools.partial(_front_kernel, tiles_per_seq=seq // tm),
        grid=(t // tm,),
        in_specs=[
            row(D_MODEL),
            _layer_spec(layer, (1, D_MODEL)),
            _layer_spec(layer, (D_MODEL, IN_MAIN)),
            _layer_spec(layer, (D_MODEL, IN_MID)),
            _layer_spec(layer, (D_MODEL, IN_GATE)),
            row(HEAD_PAD), row(HEAD_PAD),
            _layer_spec(layer, (1, MLA_Q_RANK)), _layer_spec(layer, (1, MLA_KV_RANK)),
            _layer_spec(layer, (MLA_Q_RANK, hw)), _layer_spec(layer, (MLA_KV_RANK, hw)),
            _layer_spec(layer, (MLA_KV_RANK, MLA_VAL_WIDTH)),
            _layer_spec(layer, wp.shape[1:]),
            _layer_spec(layer, (1, POOL_WIDTH)),
        ],
        out_specs=tuple(row(w) for w, _ in outs),
        out_shape=tuple(jax.ShapeDtypeStruct((t, w), dt) for w, dt in outs),
        scratch_shapes=[pltpu.VMEM((POOL_HALO, POOL_WIDTH), F32)],
        compiler_params=_params(1),
        name="front",
    )(x2, g, w_main, w_mid, w_gate, c_tab, s_tab, qn, kvn, wq, wk, wv, wp, ps)


def _gla_kernel(q_ref, k_ref, v_ref, r_ref, misc_ref, wa2_ref, ba_ref, gn_ref, o_ref):
    s_len, key_w = q_ref.shape
    c = GLA_CHUNK
    sc = GLA_SUPER
    pw = 2 * GLA_DK
    row = lax.broadcasted_iota(jnp.int32, (sc, sc), 0)
    col = lax.broadcasted_iota(jnp.int32, (sc, sc), 1)
    same_chunk = (row // c) == (col // c)
    tril = same_chunk & (row >= col)
    tri_bf = tril.astype(BF16)
    lane = lax.broadcasted_iota(jnp.int32, (1, pw), 1)
    head_lanes = [(lane >= hh * GLA_DK) & (lane < (hh + 1) * GLA_DK) for hh in range(2)]
    blk = lax.broadcasted_iota(jnp.int32, (sc, pw), 0) // c
    nsub = sc // c

    def spread(x):
        return jnp.concatenate(
            [jnp.where(blk == j, x, 0.0).astype(BF16) for j in range(nsub)], axis=1)

    blocks = [slice(n * sc, (n + 1) * sc) for n in range(s_len // sc)]
    pairs = [slice(p * pw, (p + 1) * pw) for p in range(key_w // pw)]
    vcols = lambda p, hh: slice((2 * p + hh) * GLA_DV, (2 * p + hh + 1) * GLA_DV)

    z = _dot(misc_ref[...], wa2_ref[...]) + ba_ref[...]
    la = (jnp.minimum(z, 0.0) - jnp.log1p(jnp.exp(-jnp.abs(z)))) * (1.0 / GLA_GATE_TAU)
    la3 = jnp.concatenate(_split3(la), axis=1)
    cum3 = [_dot(tri_bf, la3[rows]) for rows in blocks]

    jobs = []
    for rows, c3 in zip(blocks, cum3):
        cum_all = c3[:, :key_w] + c3[:, key_w:2 * key_w] + c3[:, 2 * key_w:]
        for p, kc in enumerate(pairs):
            cum = cum_all[:, kc]
            ends = [cum[(j + 1) * c - 1:(j + 1) * c, :] for j in range(nsub)]
            tot = jnp.concatenate([jnp.broadcast_to(r, (c, pw)) for r in ends], axis=0)
            qf = q_ref[rows, kc].astype(F32) * (GLA_DK ** -0.5)
            kf = k_ref[rows, kc].astype(F32)
            jobs.append((p, rows, qf * jnp.exp(cum), (kf * jnp.exp(-cum)).astype(BF16),
                         kf * jnp.exp(tot - cum), ends))

    upd = [sum(_dot_tn(v_ref[rows, vcols(p, hh)], spread(jnp.where(head_lanes[hh], ke, 0.0)))
               for hh in range(2)) for p, rows, _, _, ke, _ in jobs]
    att = [[jnp.where(tril, _dot_nt(jnp.where(head_lanes[hh], qd, 0.0).astype(BF16), ki),
                      0.0).astype(BF16) for hh in range(2)] for _, _, qd, ki, _, _ in jobs]

    state = [jnp.zeros((GLA_DV, pw), F32) for _ in pairs]
    st_cat = []
    for (p, _, _, _, _, ends), u in zip(jobs, upd):
        sts = []
        for j in range(nsub):
            sts.append(state[p].astype(BF16))
            state[p] = state[p] * jnp.exp(ends[j]) + u[:, j * LANES:(j + 1) * LANES]
        st_cat.append(jnp.concatenate(sts, axis=1))

    for (p, rows, qd, _, _, _), a, sc_n in zip(jobs, att, st_cat):
        for hh in range(2):
            vc = vcols(p, hh)
            o = _dot(a[hh], v_ref[rows, vc])
            o = o + _dot_nt(spread(jnp.where(head_lanes[hh], qd, 0.0)), sc_n)
            o = _rms(o, gn_ref[:, vc])
            rr = r_ref[rows, vc].astype(F32)
            o_ref[rows, vc] = (o * (rr * _sigmoid(rr))).astype(BF16)


def _gla(q, k, v, r, misc, wa2, ba, gn, layer, batch, seq):
    t = q.shape[0]
    seq_block = lambda w: pl.BlockSpec((seq, w), lambda b: (b, 0))
    return pl.pallas_call(
        _gla_kernel,
        grid=(batch,),
        in_specs=[
            seq_block(GLA_KEY_WIDTH), seq_block(GLA_KEY_WIDTH),
            seq_block(GLA_VAL_WIDTH), seq_block(GLA_VAL_WIDTH), seq_block(LANES),
            _layer_spec(layer, (LANES, GLA_KEY_WIDTH)),
            _layer_spec(layer, (1, GLA_KEY_WIDTH)),
            _layer_spec(layer, (1, GLA_VAL_WIDTH)),
        ],
        out_specs=seq_block(GLA_VAL_WIDTH),
        out_shape=jax.ShapeDtypeStruct((t, GLA_VAL_WIDTH), BF16),
        compiler_params=_params(1),
        name="gla",
    )(q, k, v, r, misc, wa2, ba, gn)


ATT_TQ = 512
ATT_HALF = ATT_TQ // 2
ATT_AHEAD = 2


def _attn_scores(q_ref, k_ref, qi, cols):
    tq, hf = ATT_TQ, ATT_HALF
    r0 = qi * tq
    q = q_ref[r0:r0 + tq, cols]
    k_d = k_ref[r0:r0 + tq, cols]
    d0 = _dot_nt(k_d[:hf], q[:hf])
    d1 = _dot_nt(k_d, q[hf:])
    s_off = _dot_nt(k_ref[0:r0, cols], q) if qi > 0 else None
    return d0, d1, s_off


def _attn_causal_masks():
    tq, hf = ATT_TQ, ATT_HALF
    key0 = lax.broadcasted_iota(jnp.int32, (hf, hf), 0)
    qry0 = lax.broadcasted_iota(jnp.int32, (hf, hf), 1)
    key1 = lax.broadcasted_iota(jnp.int32, (tq, hf), 0)
    qry1 = lax.broadcasted_iota(jnp.int32, (tq, hf), 1)
    return key0 <= qry0, key1 <= qry1 + hf


def _attn_finish(vt, ones_row, masks, qi, scores):
    tq, hf = ATT_TQ, ATT_HALF
    r0 = qi * tq
    neg = jnp.finfo(F32).min
    d0, d1, s_off = scores
    d0 = jnp.where(masks[0], d0, neg)
    d1 = jnp.where(masks[1], d1, neg)
    m0 = jnp.max(d0, axis=0, keepdims=True)
    m1 = jnp.max(d1, axis=0, keepdims=True)
    if s_off is not None:
        m0 = jnp.maximum(m0, jnp.max(s_off[:, :hf], axis=0, keepdims=True))
        m1 = jnp.maximum(m1, jnp.max(s_off[:, hf:], axis=0, keepdims=True))
        p_off0 = jnp.exp2(s_off[:, :hf] - m0)
        p_off1 = jnp.exp2(s_off[:, hf:] - m1)
        p_off = jnp.concatenate([p_off0.astype(BF16), p_off1.astype(BF16)], axis=1)
        o_off = _dot(vt[:, 0:r0], p_off)
    p0 = jnp.exp2(d0 - m0)
    p1 = jnp.exp2(d1 - m1)
    o0 = _dot(vt[:, r0:r0 + hf], p0.astype(BF16))
    o1 = _dot(vt[:, r0:r0 + tq], p1.astype(BF16))
    if s_off is not None:
        o0 = o0 + o_off[:, :hf]
        o1 = o1 + o_off[:, hf:]
    is_ones = lax.broadcasted_iota(jnp.int32, (HEAD_PAD, hf), 0) == ones_row
    return tuple(jnp.where(is_ones, 0.0, o / o[ones_row:ones_row + 1, :]) for o in (o0, o1))


def _attn_kernel(q_ref, k_ref, v_ref, o_ref):
    tq, hf = ATT_TQ, ATT_HALF
    n_pair = v_ref.shape[1] // HEAD_PAD
    head_cols = lambda p, hh: slice((2 * p + hh) * HEAD_PAD, (2 * p + hh + 1) * HEAD_PAD)
    masks = _attn_causal_masks()
    ones_rows = (MLA_V, 0)
    vt_head = []
    for p in range(n_pair):
        vt_pair = v_ref[:, p * HEAD_PAD:(p + 1) * HEAD_PAD].astype(F32).T
        vrow = lax.broadcasted_iota(jnp.int32, vt_pair.shape, 0)
        vt_head.append((
            jnp.where(vrow < MLA_V, vt_pair,
                      jnp.where(vrow == ones_rows[0], 1.0, 0.0)).astype(BF16),
            jnp.where(vrow >= MLA_V, vt_pair,
                      jnp.where(vrow == ones_rows[1], 1.0, 0.0)).astype(BF16)))
    units = [(p, qi, hh) for p in range(n_pair)
             for qi in reversed(range(q_ref.shape[0] // tq)) for hh in range(2)]
    scores_of = lambda u: _attn_scores(q_ref, k_ref, u[1], head_cols(u[0], u[2]))
    pending = [scores_of(u) for u in units[:ATT_AHEAD]]
    prev = None
    for i, (p, qi, hh) in enumerate(units):
        if i + ATT_AHEAD < len(units):
            pending.append(scores_of(units[i + ATT_AHEAD]))
        cur = _attn_finish(vt_head[p][hh], ones_rows[hh], masks, qi, pending.pop(0))
        if hh == 0:
            prev = cur
        else:
            ocols = slice(p * HEAD_PAD, (p + 1) * HEAD_PAD)
            o_ref[qi * tq:qi * tq + hf, ocols] = (prev[0] + cur[0]).T.astype(BF16)
            o_ref[qi * tq + hf:(qi + 1) * tq, ocols] = (prev[1] + cur[1]).T.astype(BF16)


def _attention(q, k, v, batch, seq):
    t = q.shape[0]
    seq_block = lambda a: pl.BlockSpec((seq, a.shape[1]), lambda b: (b, 0))
    return pl.pallas_call(
        _attn_kernel,
        grid=(batch,),
        in_specs=[seq_block(q), seq_block(k), seq_block(v)],
        out_specs=pl.BlockSpec((seq, MLA_VAL_WIDTH), lambda b: (b, 0)),
        out_shape=jax.ShapeDtypeStruct((t, MLA_VAL_WIDTH), BF16),
        compiler_params=_params(1),
        name="attention",
    )(q, k, v)


FFN_CHUNKS = ((0, 1024), (1024, 2048), (2048, D_FF))
MERGE_FFN_VMEM = 60 * 1024 * 1024
MERGE_FFN_SUB = 2


def _merge_ffn_kernel(x_ref, pa_ref, gb_ref, mc_ref, gl_ref, wa_ref, wb_ref, wc_ref, wo_ref,
                      n_mix_ref, n1_ref, wg_ref, wu_ref, wd_ref, n2_ref, o_ref):
    tm = x_ref.shape[0]
    sub = tm // MERGE_FFN_SUB
    groups = [slice(s * sub, (s + 1) * sub) for s in range(MERGE_FFN_SUB)]
    branches = ((pa_ref, wa_ref), (gb_ref, wb_ref), (mc_ref, wc_ref))

    merged = []
    for rows in groups:
        m = None
        for i, (b_ref, w_ref) in enumerate(branches):
            y = _dot(b_ref[rows, :], w_ref[...])
            gate = _sigmoid(gl_ref[rows, i * D_MODEL:(i + 1) * D_MODEL].astype(F32))
            m = gate * y if m is None else m + gate * y
        merged.append(m.astype(BF16))
    xs, hs = [], []
    for rows, m in zip(groups, merged):
        x = x_ref[rows, :] + _rms(_dot(m, wo_ref[...]), n_mix_ref[...])
        xs.append(x)
        hs.append(_rms(x, n1_ref[...]).astype(BF16))
    accs = [None] * len(groups)
    for a, b in FFN_CHUNKS:
        gu = [(_dot(h, wg_ref[:, a:b]), _dot(h, wu_ref[:, a:b])) for h in hs]
        for s, (g, u) in enumerate(gu):
            d = _dot((g * _sigmoid(g) * u).astype(BF16), wd_ref[a:b, :])
            accs[s] = d if accs[s] is None else accs[s] + d
    for rows, x, acc in zip(groups, xs, accs):
        o_ref[rows, :] = x + _rms(acc, n2_ref[...])


def _merge_ffn(x2, pa, gb, mc, glog, wa, wb, wc, wo, n_mix, n1, wgu, wd, n2, layer, tm=512):
    t = x2.shape[0]
    row = lambda w: pl.BlockSpec((tm, w), lambda i: (i, 0))
    return pl.pallas_call(
        _merge_ffn_kernel,
        grid=(t // tm,),
        in_specs=[
            row(D_MODEL), row(POOL_WIDTH), row(GLA_VAL_WIDTH), row(MLA_VAL_WIDTH),
            row(N_BRANCH * D_MODEL),
            _layer_spec(layer, (POOL_WIDTH, D_MODEL)), _layer_spec(layer, (GLA_VAL_WIDTH, D_MODEL)),
            _layer_spec(layer, (MLA_VAL_WIDTH, D_MODEL)), _layer_spec(layer, (D_MODEL, D_MODEL)),
            _layer_spec(layer, (1, D_MODEL)), _layer_spec(layer, (1, D_MODEL)),
            _layer_spec(layer, (D_MODEL, D_FF), col=0), _layer_spec(layer, (D_MODEL, D_FF), col=1),
            _layer_spec(layer, (D_FF, D_MODEL)), _layer_spec(layer, (1, D_MODEL)),
        ],
        out_specs=row(D_MODEL),
        out_shape=jax.ShapeDtypeStruct((t, D_MODEL), F32),
        compiler_params=pltpu.CompilerParams(
            dimension_semantics=("arbitrary",), vmem_limit_bytes=MERGE_FFN_VMEM),
        name="merge_ffn",
    )(x2, pa, gb, mc, glog, wa, wb, wc, wo, n_mix, n1, wgu, wgu, wd, n2)


def _prep_weights(w_in, w_gla_a2, w_mla_uq, w_mla_ukv):
    sizes = (POOL_WIDTH, GLA_KEY_WIDTH, GLA_KEY_WIDTH, GLA_VAL_WIDTH, GLA_VAL_WIDTH,
             GLA_GATE_RANK, MLA_Q_RANK, MLA_KV_RANK, MLA_ROPE, N_BRANCH * D_MODEL)
    offs = [0]
    for s in sizes:
        offs.append(offs[-1] + s)
    assert offs[5] == IN_MAIN
    w_main = w_in.astype(BF16)
    w_a1, w_cq, w_ckv, w_kr, w_gate = (w_main[:, :, offs[i]:offs[i + 1]] for i in range(5, 10))
    d = w_in.shape[0]
    z16 = jnp.zeros((d, D_MODEL, 16), BF16)
    z48 = jnp.zeros((d, D_MODEL, 48), BF16)
    w_mid = jnp.concatenate(
        [w_cq, w_ckv, w_a1, z16, w_kr[..., :16], z48, w_kr[..., 16:], z16], axis=-1)

    wa2 = jnp.pad(w_gla_a2, ((0, 0), (0, LANES - GLA_GATE_RANK), (0, 0))).astype(BF16)

    uq = w_mla_uq.reshape(d, MLA_Q_RANK, MLA_HEADS, MLA_QK)
    zq = jnp.zeros((d, MLA_Q_RANK, MLA_HEADS, 16), w_mla_uq.dtype)
    wq = jnp.concatenate(
        [uq[..., 0:32], uq[..., 64:80], zq, uq[..., 32:64], uq[..., 80:96], zq],
        axis=-1).reshape(d, MLA_Q_RANK, MLA_HEADS * HEAD_PAD).astype(BF16)

    ukv = w_mla_ukv.reshape(d, MLA_KV_RANK, MLA_HEADS, MLA_NOPE + MLA_V)
    zk = jnp.zeros((d, MLA_KV_RANK, MLA_HEADS, 32), w_mla_ukv.dtype)
    wk = jnp.concatenate(
        [ukv[..., 0:32], zk, ukv[..., 32:64], zk],
        axis=-1).reshape(d, MLA_KV_RANK, MLA_HEADS * HEAD_PAD).astype(BF16)
    wv = ukv[..., MLA_NOPE:].reshape(d, MLA_KV_RANK, MLA_VAL_WIDTH).astype(BF16)
    return w_main, w_mid, w_gate, wa2, wq, wk, wv


def kernel(x, positions, norm_pre_mix, norm_post_mix, norm_pre_ffn, norm_post_ffn, w_in, w_pool,
           pool_scale, w_a, w_gla_a2, b_gla_a, gla_norm, w_b, mla_q_norm, w_mla_uq, mla_kv_norm,
           w_mla_ukv, w_c, w_o, w_ffn_gu, w_ffn_down):
    batch, seq, d_model = x.shape
    t = batch * seq
    x2 = x.reshape(t, d_model)
    c_tab, s_tab = _rope_tables(positions)
    w_main, w_mid, w_gate, wa2, wq, wk, wv = _prep_weights(w_in, w_gla_a2, w_mla_uq, w_mla_ukv)
    w_pool_b = w_pool.astype(BF16)
    w_a_b, w_b_b, w_c_b, w_o_b = (w.astype(BF16) for w in (w_a, w_b, w_c, w_o))
    w_gu_b = w_ffn_gu.astype(BF16)
    w_d_b = w_ffn_down.astype(BF16)
    vec = lambda a: a.reshape(DEPTH, 1, -1)
    n_pre_mix, n_post_mix, n_pre_ffn, n_post_ffn = (
        vec(a) for a in (norm_pre_mix, norm_post_mix, norm_pre_ffn, norm_post_ffn))
    pool_scale, b_gla_a, gla_norm, mla_q_norm, mla_kv_norm = (
        vec(a) for a in (pool_scale, b_gla_a, gla_norm, mla_q_norm, mla_kv_norm))
    for l in range(DEPTH):
        gq, gk, gv, gr, misc, glog, pa, mq, mk, mv = _front(
            x2, n_pre_mix, w_main, w_mid, w_gate, c_tab, s_tab, mla_q_norm, mla_kv_norm,
            wq, wk, wv, w_pool_b, pool_scale, l, seq)
        gb = _gla(gq, gk, gv, gr, misc, wa2, b_gla_a, gla_norm, l, batch, seq)
        mc = _attention(mq, mk, mv, batch, seq)
        x2 = _merge_ffn(x2, pa, gb, mc, glog, w_a_b, w_b_b, w_c_b, w_o_b, n_post_mix,
                        n_pre_ffn, w_gu_b, w_d_b, n_post_ffn, l)
    return x2.reshape(batch, seq, d_model)
```
